```python
import math
import jax, jax.numpy as jnp
from jax import lax
import numpy as np

D_MODEL = 1024
BATCH = 8
SEQ = 4096
DEPTH = 2

GRID_W = 64
CTX_LEN = 256
MIX_WIDTH = 2 * D_MODEL

SSD_WIDTH = D_MODEL
SSD_HEADDIM = 64
SSD_HEADS = SSD_WIDTH // SSD_HEADDIM
SSD_GROUPS = 2
SSD_HPG = SSD_HEADS // SSD_GROUPS
SSD_STATE = 128
SSD_GN = SSD_GROUPS * SSD_STATE
SSD_CONV_DIM = SSD_WIDTH + 2 * SSD_GN
SSD_CONV_K = 5
SSD_CHUNK = 128

GLA_WIDTH = D_MODEL // 2
GLA_HEADS = 4
GLA_DK = GLA_WIDTH // (2 * GLA_HEADS)
GLA_DV = GLA_WIDTH // GLA_HEADS
GLA_RANK = 16
GLA_TAU = 16.0
GLA_CHUNK = 64

S5_WIDTH = D_MODEL // 2
S5_GROUP = 16
S5_GROUPS = S5_WIDTH // S5_GROUP
S5_STATE = 64

IN_SIZES = (SSD_WIDTH, SSD_CONV_DIM, 2 * SSD_HEADS,
            GLA_HEADS * GLA_DK, GLA_HEADS * GLA_DK, GLA_WIDTH, GLA_WIDTH, 2 * GLA_RANK,
            S5_WIDTH, S5_WIDTH)
IN_DIM = sum(IN_SIZES)
EPS = 1e-6

kernel_name = "hymba_style_ssd_gla_s5_prefix_dit"


def rmsnorm(x, w):
    xf = x.astype(jnp.float32)
    y = xf * lax.rsqrt(jnp.mean(xf * xf, axis=-1, keepdims=True) + EPS)
    return (y * w.astype(jnp.float32)).astype(x.dtype)


def split_in_proj(p):
    parts, start = [], 0
    for size in IN_SIZES:
        parts.append(p[..., start:start + size])
        start += size
    return parts


def dwconv_centred(x, w, b):
    k, ch = w.shape
    y = lax.conv_general_dilated(x, w[:, None, :].astype(x.dtype), window_strides=(1,),
                                 padding=[(k // 2, k // 2)],
                                 dimension_numbers=("NWC", "WIO", "NWC"),
                                 feature_group_count=ch)
    return y + b.astype(x.dtype)


def to_col_major(u, rows):
    bsz, length, d = u.shape
    return u.reshape(bsz, rows, GRID_W, d).transpose(0, 2, 1, 3).reshape(bsz, length, d)


def from_col_major(u, rows):
    bsz, length, d = u.shape
    return u.reshape(bsz, GRID_W, rows, d).transpose(0, 2, 1, 3).reshape(bsz, length, d)


def bidirectional(scan_fwd, scan_bwd, ctx_args, lat_args, h0):
    flip = lambda t: jnp.flip(t, axis=1)
    y_cf, h_cf = scan_fwd(*ctx_args, h0)
    y_lf, _ = scan_fwd(*lat_args, h_cf)
    y_cb, h_cb = scan_bwd(*[flip(t) for t in ctx_args], h0)
    y_lb, _ = scan_bwd(*[flip(t) for t in lat_args], h_cb)
    return y_cf + flip(y_cb), y_lf + flip(y_lb)


def segsum(a):
    t = a.shape[-1]
    cs = jnp.cumsum(a, axis=-1)
    diff = cs[..., :, None] - cs[..., None, :]
    return jnp.where(jnp.tril(jnp.ones((t, t), bool)), diff, -jnp.inf)


def ssd_scan(x, da, bm, cm, h0):
    bsz, length, g, r, p = x.shape
    n = bm.shape[-1]
    nc = length // SSD_CHUNK
    x = x.reshape(bsz, nc, SSD_CHUNK, g, r, p)
    bm = bm.reshape(bsz, nc, SSD_CHUNK, g, n)
    cm = cm.reshape(bsz, nc, SSD_CHUNK, g, n)
    da = da.reshape(bsz, nc, SSD_CHUNK, g, r).transpose(0, 3, 4, 1, 2)
    acs = jnp.cumsum(da, axis=-1)
    decay = jnp.exp(segsum(da))
    cb = jnp.einsum("bclgn,bcsgn->bgcls", cm, bm)
    y_diag = jnp.einsum("bgrcls,bcsgrp->bclgrp", cb[:, :, None] * decay, x)
    decay_states = jnp.exp(acs[..., -1:] - acs)
    states = jnp.einsum("bclgn,bgrcl,bclgrp->bcgrpn", bm, decay_states, x)

    def step(h, inp):
        dec, st = inp
        return h * dec[..., None, None] + st, h

    h_fin, h_prev = lax.scan(step, h0, (jnp.exp(acs[..., -1]).transpose(3, 0, 1, 2),
                                        states.transpose(1, 0, 2, 3, 4, 5)))
    y_off = jnp.einsum("bclgn,cbgrpn,bgrcl->bclgrp", cm, h_prev, jnp.exp(acs))
    return (y_diag + y_off).reshape(bsz, length, g, r, p), h_fin


def gla_scan(q, k, v, g, s0):
    bsz, length, h, dk = q.shape
    dv = v.shape[-1]
    nc = length // GLA_CHUNK
    q = q.reshape(bsz, nc, GLA_CHUNK, h, dk)
    k = k.reshape(bsz, nc, GLA_CHUNK, h, dk)
    v = v.reshape(bsz, nc, GLA_CHUNK, h, dv)
    b = jnp.cumsum(g.reshape(bsz, nc, GLA_CHUNK, h, dk), axis=2)
    b_last = b[:, :, -1]
    qe = q * jnp.exp(b)
    ke = k * jnp.exp(-b)
    kd = k * jnp.exp(b_last[:, :, None] - b)
    lower = jnp.tril(jnp.ones((GLA_CHUNK, GLA_CHUNK), bool))
    attn = jnp.where(lower, jnp.einsum("bcthd,bcshd->bchts", qe, ke), 0.0)
    o = jnp.einsum("bchts,bcshv->bcthv", attn, v)
    upd = jnp.einsum("bcshd,bcshv->bchdv", kd, v)

    def step(s, inp):
        dec, u = inp
        return s * dec[..., None] + u, s

    s_fin, s_prev = lax.scan(step, s0, (jnp.exp(b_last).transpose(1, 0, 2, 3),
                                        upd.transpose(1, 0, 2, 3, 4)))
    o = o + jnp.einsum("bcthd,cbhdv->bcthv", qe, s_prev)
    return o.reshape(bsz, length, h, dv), s_fin


def s5_scan(bu, h0, lam_bar):
    bu = bu.at[:, 0].add(lam_bar * h0)
    a = jnp.broadcast_to(lam_bar, bu.shape)

    def combine(e1, e2):
        a1, b1 = e1
        a2, b2 = e2
        return a1 * a2, a2 * b1 + b2

    _, h = lax.associative_scan(combine, (a, bu), axis=1)
    return h


def ssd_branch(z_c, xbc_c, dt_c, z_l, xbc_l, dt_l, conv_w, conv_b, a_log, dt_bias, d_skip, norm_w):
    f32 = jnp.float32
    a_neg = -jnp.exp(a_log.astype(f32))

    def prep(xbc, dt):
        bsz, length = xbc.shape[:2]
        xbc = jax.nn.silu(dwconv_centred(xbc, conv_w, conv_b)).astype(f32)
        xs = xbc[..., :SSD_WIDTH].reshape(bsz, length, SSD_GROUPS, SSD_HPG, SSD_HEADDIM)
        bm = xbc[..., SSD_WIDTH:SSD_WIDTH + SSD_GN].reshape(bsz, length, SSD_GROUPS, SSD_STATE)
        cm = xbc[..., SSD_WIDTH + SSD_GN:].reshape(bsz, length, SSD_GROUPS, SSD_STATE)
        dt = jax.nn.softplus(dt.astype(f32).reshape(bsz, length, 2, SSD_HEADS) + dt_bias.astype(f32))
        return xs, bm, cm, dt.reshape(bsz, length, 2, SSD_GROUPS, SSD_HPG)

    def make_scan(d):
        a_d = a_neg[d].reshape(SSD_GROUPS, SSD_HPG)

        def scan_fn(xs, bm, cm, dt, h0):
            dt_d = dt[:, :, d]
            return ssd_scan(xs * dt_d[..., None], dt_d * a_d, bm, cm, h0)
        return scan_fn

    ctx_args = prep(xbc_c, dt_c)
    lat_args = prep(xbc_l, dt_l)
    h0 = jnp.zeros((xbc_c.shape[0], SSD_GROUPS, SSD_HPG, SSD_HEADDIM, SSD_STATE), f32)
    y_c, y_l = bidirectional(make_scan(0), make_scan(1), ctx_args, lat_args, h0)
    dsk = d_skip.astype(f32).reshape(SSD_GROUPS, SSD_HPG, 1)

    def finish(y, xs, z):
        bsz, length = z.shape[:2]
        y = (y + dsk * xs).reshape(bsz, length, SSD_WIDTH) * jax.nn.silu(z.astype(f32))
        y = y.reshape(bsz, length, SSD_GROUPS, SSD_WIDTH // SSD_GROUPS)
        y = y * lax.rsqrt(jnp.mean(y * y, axis=-1, keepdims=True) + EPS)
        return (y.reshape(bsz, length, SSD_WIDTH) * norm_w.astype(f32)).astype(z.dtype)

    return finish(y_c, ctx_args[0], z_c), finish(y_l, lat_args[0], z_l)


def gla_branch(q_c, k_c, v_c, lr_c, gate_c, q_l, k_l, v_l, lr_l, gate_l, w_lr, b_lr, norm_w):
    f32 = jnp.float32

    def prep(q, k, v, lr):
        bsz, length = q.shape[:2]
        q = q.astype(f32).reshape(bsz, length, GLA_HEADS, GLA_DK) * GLA_DK ** -0.5
        k = k.astype(f32).reshape(bsz, length, GLA_HEADS, GLA_DK)
        v = v.astype(f32).reshape(bsz, length, GLA_HEADS, GLA_DV)
        lr = lr.astype(f32).reshape(bsz, length, 2, GLA_RANK)
        logit = jnp.einsum("blxr,xrk->blxk", lr, w_lr.astype(f32)) + b_lr.astype(f32)
        g = jax.nn.log_sigmoid(logit) / GLA_TAU
        return q, k, v, g.reshape(bsz, length, 2, GLA_HEADS, GLA_DK)

    def make_scan(d):
        def scan_fn(q, k, v, g, s0):
            return gla_scan(q, k, v, g[:, :, d], s0)
        return scan_fn

    ctx_args = prep(q_c, k_c, v_c, lr_c)
    lat_args = prep(q_l, k_l, v_l, lr_l)
    s0 = jnp.zeros((q_c.shape[0], GLA_HEADS, GLA_DK, GLA_DV), f32)
    o_c, o_l = bidirectional(make_scan(0), make_scan(1), ctx_args, lat_args, s0)

    def finish(o, gate):
        bsz, length = gate.shape[:2]
        o = o * lax.rsqrt(jnp.mean(o * o, axis=-1, keepdims=True) + EPS) * norm_w.astype(f32)
        return (o.reshape(bsz, length, GLA_WIDTH) * jax.nn.silu(gate.astype(f32))).astype(gate.dtype)

    return finish(o_c, gate_c), finish(o_l, gate_l)


def s5_branch(u_c, gate_c, u_l, gate_l, lam_re, lam_im, log_step, b_re, b_im, c_re, c_im,
              d_skip, glu_w, glu_b):
    f32 = jnp.float32
    bmat = lax.complex(b_re.astype(f32), b_im.astype(f32))

    def make_scan(d):
        lam = lax.complex(lam_re[d].astype(f32), lam_im[d].astype(f32))
        lam_bar = jnp.exp(lam * jnp.exp(log_step[d].astype(f32))[:, None])
        b_bar = ((lam_bar - 1.0) / lam)[..., None] * bmat
        cmat = lax.complex(c_re[d].astype(f32), c_im[d].astype(f32))

        def scan_fn(u, h0):
            bu = jnp.einsum("blgh,gph->blgp", u.astype(jnp.complex64), b_bar)
            h = s5_scan(bu, h0, lam_bar)
            return jnp.einsum("blgp,ghp->blgh", h, cmat).real, h[:, -1]
        return scan_fn

    def prep(u):
        bsz, length = u.shape[:2]
        return (u.astype(f32).reshape(bsz, length, S5_GROUPS, S5_GROUP),)

    ctx_args = prep(u_c)
    lat_args = prep(u_l)
    h0 = jnp.zeros((u_c.shape[0], S5_GROUPS, S5_STATE), jnp.complex64)
    y_c, y_l = bidirectional(make_scan(0), make_scan(1), ctx_args, lat_args, h0)
    dsk = d_skip.astype(f32).reshape(S5_GROUPS, S5_GROUP)

    def finish(y, u, gate):
        bsz, length = gate.shape[:2]
        y = jax.nn.gelu((y + dsk * u).reshape(bsz, length, S5_WIDTH))
        pr = y @ glu_w.astype(f32) + glu_b.astype(f32)
        y = pr[..., :S5_WIDTH] * jax.nn.sigmoid(pr[..., S5_WIDTH:])
        return (y * jax.nn.silu(gate.astype(f32))).astype(gate.dtype)

    return finish(y_c, ctx_args[0], gate_c), finish(y_l, lat_args[0], gate_l)


def setup_inputs(seed: int = 0) -> dict:
    key = jax.random.key(seed)
    ks = iter(jax.random.split(key, 40))
    f32 = jnp.float32
    nrm = lambda shape, s: s * jax.random.normal(next(ks), shape, f32)
    x = nrm((BATCH, SEQ, D_MODEL), 1.0)
    c = nrm((BATCH, D_MODEL), 1.0)
    ctx = nrm((BATCH, CTX_LEN, D_MODEL), 1.0)
    c_ctx = nrm((D_MODEL,), 1.0)
    norm_w = 1.0 + nrm((DEPTH, D_MODEL), 0.02)
    mod_w = nrm((DEPTH, D_MODEL, 3 * D_MODEL), 0.5 * D_MODEL ** -0.5)
    mod_b = nrm((DEPTH, 3 * D_MODEL), 0.02)
    w_in = nrm((DEPTH, D_MODEL, IN_DIM), D_MODEL ** -0.5)
    w_out = nrm((DEPTH, MIX_WIDTH, D_MODEL), MIX_WIDTH ** -0.5)
    ssd_conv_w = nrm((DEPTH, SSD_CONV_K, SSD_CONV_DIM), SSD_CONV_K ** -0.5)
    ssd_conv_b = nrm((DEPTH, SSD_CONV_DIM), 0.02)
    ssd_a_log = jnp.log(jax.random.uniform(next(ks), (DEPTH, 2, SSD_HEADS), f32, 1.0, 16.0))
    dt0 = jnp.exp(jax.random.uniform(next(ks), (DEPTH, 2, SSD_HEADS), f32,
                                     math.log(1e-3), math.log(1e-1)))
    ssd_dt_bias = dt0 + jnp.log(-jnp.expm1(-dt0))
    ssd_d = 1.0 + nrm((DEPTH, SSD_HEADS), 0.1)
    ssd_norm_w = 1.0 + nrm((DEPTH, SSD_WIDTH), 0.02)
    gla_w_lr = nrm((DEPTH, 2, GLA_RANK, GLA_HEADS * GLA_DK), GLA_RANK ** -0.5)
    gla_b_lr = nrm((DEPTH, 2, GLA_HEADS * GLA_DK), 0.5)
    gla_norm_w = 1.0 + nrm((DEPTH, GLA_DV), 0.02)
    s5_lam_re = -0.5 + nrm((DEPTH, 2, S5_GROUPS, S5_STATE), 0.01)
    s5_lam_im = jnp.pi * jnp.arange(S5_STATE, dtype=f32) + nrm((DEPTH, 2, S5_GROUPS, S5_STATE), 0.01)
    s5_log_step = jax.random.uniform(next(ks), (DEPTH, 2, S5_GROUPS), f32,
                                     math.log(1e-3), math.log(1e-1))
    s5_b_re = nrm((DEPTH, S5_GROUPS, S5_STATE, S5_GROUP), (2 * S5_GROUP) ** -0.5)
    s5_b_im = nrm((DEPTH, S5_GROUPS, S5_STATE, S5_GROUP), (2 * S5_GROUP) ** -0.5)
    s5_c_re = nrm((DEPTH, 2, S5_GROUPS, S5_GROUP, S5_STATE), (2 * S5_STATE) ** -0.5)
    s5_c_im = nrm((DEPTH, 2, S5_GROUPS, S5_GROUP, S5_STATE), (2 * S5_STATE) ** -0.5)
    s5_d = nrm((DEPTH, S5_WIDTH), 1.0)
    s5_glu_w = nrm((DEPTH, S5_WIDTH, 2 * S5_WIDTH), S5_WIDTH ** -0.5)
    s5_glu_b = nrm((DEPTH, 2 * S5_WIDTH), 0.02)
    final_norm_w = 1.0 + nrm((D_MODEL,), 0.02)
    return {"x": x, "c": c, "ctx": ctx, "c_ctx": c_ctx, "norm_w": norm_w, "mod_w": mod_w,
            "mod_b": mod_b, "w_in": w_in, "w_out": w_out, "ssd_conv_w": ssd_conv_w,
            "ssd_conv_b": ssd_conv_b, "ssd_a_log": ssd_a_log, "ssd_dt_bias": ssd_dt_bias,
            "ssd_d": ssd_d, "ssd_norm_w": ssd_norm_w, "gla_w_lr": gla_w_lr, "gla_b_lr": gla_b_lr,
            "gla_norm_w": gla_norm_w, "s5_lam_re": s5_lam_re, "s5_lam_im": s5_lam_im,
            "s5_log_step": s5_log_step, "s5_b_re": s5_b_re, "s5_b_im": s5_b_im,
            "s5_c_re": s5_c_re, "s5_c_im": s5_c_im, "s5_d": s5_d, "s5_glu_w": s5_glu_w,
            "s5_glu_b": s5_glu_b, "final_norm_w": final_norm_w}


def reference(x, c, ctx, c_ctx, norm_w, mod_w, mod_b, w_in, w_out, ssd_conv_w, ssd_conv_b,
              ssd_a_log, ssd_dt_bias, ssd_d, ssd_norm_w, gla_w_lr, gla_b_lr, gla_norm_w,
              s5_lam_re, s5_lam_im, s5_log_step, s5_b_re, s5_b_im, s5_c_re, s5_c_im, s5_d,
              s5_glu_w, s5_glu_b, final_norm_w):
    length = x.shape[1]
    rows = length // GRID_W
    h_lat, h_ctx = x, ctx
    for l in range(DEPTH):
        col_major = (l % 2 == 1)
        mod = jax.nn.silu(c) @ mod_w[l] + mod_b[l]
        shift, scale, gate = jnp.split(mod, 3, axis=-1)
        mod_c = jax.nn.silu(c_ctx) @ mod_w[l] + mod_b[l]
        shift_c, scale_c, gate_c = jnp.split(mod_c, 3, axis=-1)
        u_lat = rmsnorm(h_lat, norm_w[l]) * (1.0 + scale[:, None]) + shift[:, None]
        u_ctx = rmsnorm(h_ctx, norm_w[l]) * (1.0 + scale_c) + shift_c
        if col_major:
            u_lat = to_col_major(u_lat, rows)
        (z_c, xbc_c, dt_c, q_c, k_c, v_c, gg_c, lr_c, u5_c, sg_c) = split_in_proj(u_ctx @ w_in[l])
        (z_l, xbc_l, dt_l, q_l, k_l, v_l, gg_l, lr_l, u5_l, sg_l) = split_in_proj(u_lat @ w_in[l])
        ssd_c, ssd_l = ssd_branch(z_c, xbc_c, dt_c, z_l, xbc_l, dt_l, ssd_conv_w[l], ssd_conv_b[l],
                                  ssd_a_log[l], ssd_dt_bias[l], ssd_d[l], ssd_norm_w[l])
        gla_c, gla_l = gla_branch(q_c, k_c, v_c, lr_c, gg_c, q_l, k_l, v_l, lr_l, gg_l,
                                  gla_w_lr[l], gla_b_lr[l], gla_norm_w[l])
        s5_c, s5_l = s5_branch(u5_c, sg_c, u5_l, sg_l, s5_lam_re[l], s5_lam_im[l], s5_log_step[l],
                               s5_b_re[l], s5_b_im[l], s5_c_re[l], s5_c_im[l], s5_d[l],
                               s5_glu_w[l], s5_glu_b[l])
        out_lat = jnp.concatenate([ssd_l, gla_l, s5_l], axis=-1) @ w_out[l]
        if col_major:
            out_lat = from_col_major(out_lat, rows)
        h_lat = h_lat + gate[:, None] * out_lat
        if l < DEPTH - 1:
            out_ctx = jnp.concatenate([ssd_c, gla_c, s5_c], axis=-1) @ w_out[l]
            h_ctx = h_ctx + gate_c * out_ctx
    return rmsnorm(h_lat, final_norm_w)
```

```python
import functools

import jax
import jax.numpy as jnp
from jax import lax
from jax.experimental import pallas as pl
from jax.experimental.pallas import tpu as pltpu

F32, BF16 = jnp.float32, jnp.bfloat16
HIGHEST = lax.Precision.HIGHEST
EPS = 1e-6

D_MODEL = 1024
GRID_W = 64
CTX_LEN = 256
TILE = 256
LANES = 128
VMEM_LIMIT = 56 * 1024 * 1024

SSD_WIDTH, SSD_HEADS, SSD_HEADDIM, SSD_STATE, SSD_CHUNK = 1024, 16, 64, 128, 128
SSD_CONV_DIM, SSD_CONV_K = 1536, 5
GLA_HEADS, GLA_DK, GLA_DV, GLA_RANK, GLA_TAU, GLA_CHUNK = 4, 64, 128, 16, 16.0, 64
GLA_WIDTH = GLA_HEADS * GLA_DV
S5_WIDTH, S5_GROUP, S5_GROUPS, S5_STATE = 512, 16, 32, 64
S5_T = LANES // S5_GROUP
MIX_WIDTH = SSD_WIDTH + GLA_WIDTH + S5_WIDTH

C_Z, C_XBC, C_DT = 0, 1024, 2560
C_QKV, C_GG, C_LR = 2816, 3840, 4352
C_U5, C_SG, C_END = 4608, 5120, 5632


def _params(semantics):
    return pltpu.CompilerParams(dimension_semantics=semantics, vmem_limit_bytes=VMEM_LIMIT)


def _silu(v):
    return v * jax.nn.sigmoid(v)


def _dot(a, b):
    return jnp.dot(a, b, preferred_element_type=F32)


def _dot_nt(a, b):
    return lax.dot_general(a, b, (((1,), (1,)), ((), ())), preferred_element_type=F32)


def _dot_exact_lhs(m_bf16, v):
    hi = v.astype(BF16)
    r1 = v - hi.astype(F32)
    mid = r1.astype(BF16)
    lo = (r1 - mid.astype(F32)).astype(BF16)
    return _dot(m_bf16, hi) + _dot(m_bf16, mid) + _dot(m_bf16, lo)


def _iota(shape, dim):
    return lax.broadcasted_iota(jnp.int32, shape, dim)


def _mod_body(cc_ref, w_ref, b_ref, o_ref):
    s = _silu(cc_ref[...])
    o_ref[0] = jnp.dot(s, w_ref[0], precision=HIGHEST, preferred_element_type=F32) + b_ref[0]


def _modulation(cc, mod_w, mod_b):
    depth, d, three_d = mod_w.shape
    rows = cc.shape[0]
    return pl.pallas_call(
        _mod_body,
        grid=(depth, three_d // d),
        in_specs=[pl.BlockSpec((rows, d), lambda l, n: (0, 0)),
                  pl.BlockSpec((1, d, d), lambda l, n: (l, 0, n)),
                  pl.BlockSpec((1, 1, d), lambda l, n: (l, 0, n))],
        out_specs=pl.BlockSpec((1, rows, d), lambda l, n: (l, 0, n)),
        out_shape=jax.ShapeDtypeStruct((depth, rows, three_d), F32),
        compiler_params=_params(("arbitrary", "arbitrary")),
        name="modulation",
    )(cc, mod_w, mod_b.reshape(depth, 1, three_d))


def _load_tile(lat_ref, ctx_ref, col_major, is_ctx):
    if col_major:
        lat = jnp.concatenate(
            [lat_ref[0, :, k * D_MODEL:(k + 1) * D_MODEL] for k in range(lat_ref.shape[2] // D_MODEL)], axis=0)
    else:
        lat = lat_ref[0]
    if ctx_ref is None:
        return lat
    return jnp.where(is_ctx, ctx_ref[0], lat)


def _token_specs(n_lat_tiles, rows, col_major, with_ctx, ctx_block):
    last = n_lat_tiles - 1
    if col_major:
        cols = TILE // rows
        lat = pl.BlockSpec((1, rows, cols * D_MODEL), lambda b, j: (b, 0, jnp.minimum(j, last)))
    else:
        lat = pl.BlockSpec((1, TILE, D_MODEL), lambda b, j: (b, jnp.minimum(j, last), 0))
    if not with_ctx:
        return [lat]
    return [lat, pl.BlockSpec((1, TILE, D_MODEL), lambda b, j: (b, ctx_block, 0))]


def _inproj_body(*refs, col_major, n_batch, n_lat_tiles):
    lat_ref, ctx_ref, mod_ref, nw_ref, w_ref = refs[:5]
    z_ref, xbc_ref, dt_ref, qkv_ref, gg_ref, lr_ref, u5_ref, sg_ref = refs[5:]
    b, j = pl.program_id(0), pl.program_id(1)
    is_ctx = j == n_lat_tiles
    x = _load_tile(lat_ref, ctx_ref, col_major, is_ctx)
    m = mod_ref[pl.ds(jnp.where(is_ctx, n_batch, b), 1), :]
    shift, scale = m[:, :D_MODEL], m[:, D_MODEL:2 * D_MODEL]
    u = x * lax.rsqrt(jnp.mean(x * x, axis=-1, keepdims=True) + EPS) * nw_ref[...]
    u = (u * (1.0 + scale) + shift).astype(BF16)
    z_ref[0] = _dot(u, w_ref[:, C_Z:C_XBC])
    xbc_ref[0] = _dot(u, w_ref[:, C_XBC:C_DT])
    dt_ref[0] = _dot(u, w_ref[:, C_DT:C_QKV])
    qkv_ref[0] = _dot(u, w_ref[:, C_QKV:C_GG])
    gg_ref[0] = _dot(u, w_ref[:, C_GG:C_LR])
    lr_ref[0] = _dot(u, w_ref[:, C_LR:C_U5])
    u5 = _dot(u, w_ref[:, C_U5:C_SG])
    for k in range(S5_WIDTH // LANES):
        u5_ref[0, k] = u5[:, k * LANES:(k + 1) * LANES]
    sg_ref[0] = _dot(u, w_ref[:, C_SG:C_END])


def _inproj(lat, ctx, ctx_block, mod_l, norm_w, w_packed, *, col_major, n_batch, seq, rows):
    n_lat_tiles = seq // TILE
    lt = seq + CTX_LEN
    tok = lambda width: pl.BlockSpec((1, TILE, width), lambda b, j: (b, j, 0))
    full = lambda a: pl.BlockSpec(a.shape, lambda b, j: (0,) * a.ndim)
    shape = lambda width: jax.ShapeDtypeStruct((n_batch, lt, width), F32)
    return pl.pallas_call(
        functools.partial(_inproj_body, col_major=col_major, n_batch=n_batch, n_lat_tiles=n_lat_tiles),
        grid=(n_batch, n_lat_tiles + 1),
        in_specs=_token_specs(n_lat_tiles, rows, col_major, True, ctx_block)
        + [full(mod_l), full(norm_w), full(w_packed)],
        out_specs=[tok(1024), tok(SSD_CONV_DIM), tok(256), tok(1024), tok(512), tok(256),
                   pl.BlockSpec((1, S5_WIDTH // LANES, TILE, LANES), lambda b, j: (b, 0, j, 0)),
                   tok(512)],
        out_shape=[shape(1024), shape(SSD_CONV_DIM), shape(256), shape(1024), shape(512), shape(256),
                   jax.ShapeDtypeStruct((n_batch, S5_WIDTH // LANES, lt, LANES), F32), shape(512)],
        compiler_params=_params(("arbitrary", "arbitrary")),
        name="inproj_col" if col_major else "inproj_row",
    )(lat, ctx, mod_l, norm_w, w_packed)


CONV_ROWS = 256
CONV_COLS = 256
HALO = 8


def _conv_body(x_ref, w_ref, b_ref, o_ref, *, seq, total):
    n_chunks = total // CONV_ROWS
    lat_chunks = seq // CONV_ROWS
    w = w_ref[...]
    bias = b_ref[...]

    def chunk(i, carry):
        r0 = pl.multiple_of(i * CONV_ROWS, CONV_ROWS)
        main = x_ref[0, pl.ds(r0, CONV_ROWS), :]
        at_start = (i == 0) | (i == lat_chunks)
        at_end = (i == lat_chunks - 1) | (i == n_chunks - 1)
        p0 = pl.multiple_of(jnp.maximum(r0 - HALO, 0), HALO)
        n0 = pl.multiple_of(jnp.minimum(r0 + CONV_ROWS, total - HALO), HALO)
        prev = jnp.where(at_start, 0.0, x_ref[0, pl.ds(p0, HALO), :])
        nxt = jnp.where(at_end, 0.0, x_ref[0, pl.ds(n0, HALO), :])
        ext = jnp.concatenate([prev, main, nxt], axis=0)
        acc = jnp.broadcast_to(bias, main.shape)
        for k in range(SSD_CONV_K):
            off = HALO + k - SSD_CONV_K // 2
            acc = acc + w[k:k + 1, :] * ext[off:off + CONV_ROWS, :]
        o_ref[0, pl.ds(r0, CONV_ROWS), :] = _silu(acc)
        return carry

    lax.fori_loop(0, n_chunks, chunk, 0)


def _conv(xbc, conv_w, conv_b, *, seq):
    n_batch, total, ch = xbc.shape
    return pl.pallas_call(
        functools.partial(_conv_body, seq=seq, total=total),
        grid=(n_batch, ch // CONV_COLS),
        in_specs=[pl.BlockSpec((1, total, CONV_COLS), lambda b, c: (b, 0, c)),
                  pl.BlockSpec((SSD_CONV_K, CONV_COLS), lambda b, c: (0, c)),
                  pl.BlockSpec((1, CONV_COLS), lambda b, c: (0, c))],
        out_specs=pl.BlockSpec((1, total, CONV_COLS), lambda b, c: (b, 0, c)),
        out_shape=jax.ShapeDtypeStruct(xbc.shape, F32),
        compiler_params=_params(("arbitrary", "arbitrary")),
        name="ssd_conv",
    )(xbc, conv_w, conv_b.reshape(1, ch))


def _ssd_body(xc_ref, dt_ref, bias_ref, alog_ref, dsk_ref, y_ref, st_ref):
    d, i = pl.program_id(1), pl.program_id(2)
    fwd = d == 0

    @pl.when(i == 0)
    def _():
        st_ref[...] = jnp.zeros_like(st_ref)

    t = SSD_CHUNK
    dt = jax.nn.softplus(dt_ref[0] + bias_ref[pl.ds(d, 1), :])
    da = dt * -jnp.exp(alog_ref[pl.ds(d, 1), :])
    row, col = _iota((t, t), 0), _iota((t, t), 1)
    tri = jnp.where(fwd, row - col, col - row) >= 0
    p = _dot_exact_lhs(jnp.where(tri, 1.0, 0.0).astype(BF16), da)
    p_t = p.T
    total = jnp.sum(da, axis=0, keepdims=True)
    w_dec = jnp.exp(total - p)
    e_p = jnp.exp(p)
    e_tot = jnp.exp(total)
    lo = _iota((1, LANES), 1) < SSD_HEADDIM
    pairs_per_group = SSD_HEADS // 2 // 2
    for g in range(2):
        bg = xc_ref[0, :, SSD_WIDTH + g * SSD_STATE:SSD_WIDTH + (g + 1) * SSD_STATE]
        cg = xc_ref[0, :, SSD_WIDTH + (2 + g) * SSD_STATE:SSD_WIDTH + (3 + g) * SSD_STATE]
        cb = _dot_nt(cg.astype(BF16), bg.astype(BF16))
        bg_t = bg.T.astype(BF16)
        for q in range(pairs_per_group):
            hp = g * pairs_per_group + q
            h0, h1 = 2 * hp, 2 * hp + 1
            xp = xc_ref[0, :, hp * LANES:(hp + 1) * LANES]
            xdt = xp * jnp.where(lo, dt[:, h0:h0 + 1], dt[:, h1:h1 + 1])
            xw = (xdt * jnp.where(lo, w_dec[:, h0:h0 + 1], w_dec[:, h1:h1 + 1])).astype(BF16)
            st = st_ref[hp]
            acc = jnp.where(fwd, dsk_ref[:, hp * LANES:(hp + 1) * LANES], 0.0) * xp
            for h, sel in ((h0, lo), (h1, jnp.logical_not(lo))):
                decay = jnp.where(tri, jnp.exp(p[:, h:h + 1] - p_t[h:h + 1, :]), 0.0)
                lhs = jnp.concatenate([(cb * decay).astype(BF16),
                                       (cg * e_p[:, h:h + 1]).astype(BF16)], axis=1)
                rhs = jnp.concatenate([jnp.where(sel, xdt, 0.0).astype(BF16),
                                       jnp.where(sel, st, 0.0).astype(BF16)], axis=0)
                acc = acc + _dot(lhs, rhs)
            y_ref[0, 0, :, hp * LANES:(hp + 1) * LANES] = acc
            dec = jnp.where(lo, e_tot[:, h0:h0 + 1], e_tot[:, h1:h1 + 1])
            st_ref[hp] = dec * st + _dot(bg_t, xw)


def _ssd_scan(xc, dt, dt_bias, a_log, dsk, *, seq):
    n_batch, total, _ = xc.shape
    nc = total // SSD_CHUNK
    lat_c = seq // SSD_CHUNK
    ctx_c = nc - lat_c

    def chunk(d, i):
        fwd_c = jnp.where(i < ctx_c, lat_c + i, i - ctx_c)
        return jnp.where(d == 0, fwd_c, nc - 1 - i)

    full = lambda a: pl.BlockSpec(a.shape, lambda b, d, i: (0,) * a.ndim)
    return pl.pallas_call(
        _ssd_body,
        grid=(n_batch, 2, nc),
        in_specs=[pl.BlockSpec((1, SSD_CHUNK, SSD_CONV_DIM), lambda b, d, i: (b, chunk(d, i), 0)),
                  pl.BlockSpec((1, SSD_CHUNK, LANES), lambda b, d, i: (b, chunk(d, i), d)),
                  full(dt_bias), full(a_log), full(dsk)],
        out_specs=pl.BlockSpec((1, 1, SSD_CHUNK, SSD_WIDTH), lambda b, d, i: (d, b, chunk(d, i), 0)),
        out_shape=jax.ShapeDtypeStruct((2, n_batch, total, SSD_WIDTH), F32),
        scratch_shapes=[pltpu.VMEM((SSD_HEADS // 2, SSD_STATE, LANES), F32)],
        compiler_params=_params(("arbitrary", "arbitrary", "arbitrary")),
        name="ssd_scan",
    )(xc, dt, dt_bias, a_log, dsk)


def _tile_index(d, i, n_lat_tiles):
    return jnp.where(i == 0, n_lat_tiles, jnp.where(d == 0, i - 1, n_lat_tiles - i))


def _gla_body(qkv_ref, lr_ref, wlr_ref, blr_ref, o_ref, s_ref):
    d, i = pl.program_id(1), pl.program_id(2)
    fwd = d == 0

    @pl.when(i == 0)
    def _():
        s_ref[...] = jnp.zeros_like(s_ref)

    t = GLA_CHUNK
    n_chunks = TILE // t
    row, col = _iota((t, t), 0), _iota((t, t), 1)
    tri = jnp.where(fwd, row - col, col - row) >= 0
    tri_b = jnp.where(tri, 1.0, 0.0).astype(BF16)
    lo = _iota((1, LANES), 1) < GLA_DK
    eye = _iota((LANES, LANES), 0) == _iota((LANES, LANES), 1)
    row_lo = _iota((LANES, 1), 0) < GLA_DK
    kw = GLA_HEADS * GLA_DK
    for step in range(n_chunks):
        r0 = pl.multiple_of(jnp.where(fwd, step, n_chunks - 1 - step) * t, t)
        rows = pl.ds(r0, t)
        logit = jnp.dot(lr_ref[0, rows, :], wlr_ref[0], precision=HIGHEST,
                        preferred_element_type=F32) + blr_ref[0]
        g = jax.nn.log_sigmoid(logit) / GLA_TAU
        p = _dot_exact_lhs(tri_b, g)
        total = jnp.sum(g, axis=0, keepdims=True)
        q = qkv_ref[0, rows, 0:kw] * GLA_DK ** -0.5
        k = qkv_ref[0, rows, kw:2 * kw]
        qe = q * jnp.exp(p)
        ke = k * jnp.exp(-p)
        kd = k * jnp.exp(total - p)
        e_tot = jnp.exp(total)
        for pr in range(GLA_HEADS // 2):
            sl = slice(pr * LANES, (pr + 1) * LANES)
            qe_p, ke_p = qe[:, sl], ke[:, sl].astype(BF16)
            kd_t = kd[:, sl].T.astype(BF16)
            s = s_ref[pr]
            s_b = s.astype(BF16)
            dec_col = jnp.sum(jnp.where(eye, e_tot[:, sl], 0.0), axis=1, keepdims=True)
            upd = []
            for e, sel in ((0, lo), (1, jnp.logical_not(lo))):
                h = 2 * pr + e
                qm = jnp.where(sel, qe_p, 0.0).astype(BF16)
                attn = jnp.where(tri, _dot_nt(qm, ke_p), 0.0).astype(BF16)
                v = qkv_ref[0, rows, 2 * kw + h * GLA_DV:2 * kw + (h + 1) * GLA_DV].astype(BF16)
                o_ref[0, 0, rows, h * GLA_DV:(h + 1) * GLA_DV] = _dot(attn, v) + _dot(qm, s_b)
                upd.append(_dot(kd_t, v))
            s_ref[pr] = dec_col * s + jnp.where(row_lo, upd[0], upd[1])


def _gla_scan(qkv, lr, w_lr, b_lr, *, seq):
    n_batch, total, _ = qkv.shape
    n_lat_tiles = seq // TILE
    tile = lambda d, i: _tile_index(d, i, n_lat_tiles)
    return pl.pallas_call(
        _gla_body,
        grid=(n_batch, 2, n_lat_tiles + 1),
        in_specs=[pl.BlockSpec((1, TILE, 1024), lambda b, d, i: (b, tile(d, i), 0)),
                  pl.BlockSpec((1, TILE, LANES), lambda b, d, i: (b, tile(d, i), d)),
                  pl.BlockSpec((1, LANES, GLA_HEADS * GLA_DK), lambda b, d, i: (d, 0, 0)),
                  pl.BlockSpec((1, 1, GLA_HEADS * GLA_DK), lambda b, d, i: (d, 0, 0))],
        out_specs=pl.BlockSpec((1, 1, TILE, GLA_WIDTH), lambda b, d, i: (d, b, tile(d, i), 0)),
        out_shape=jax.ShapeDtypeStruct((2, n_batch, total, GLA_WIDTH), F32),
        scratch_shapes=[pltpu.VMEM((GLA_HEADS // 2, LANES, GLA_DV), F32)],
        compiler_params=_params(("arbitrary", "arbitrary", "arbitrary")),
        name="gla_scan",
    )(qkv, lr, w_lr, b_lr)


S5_ROWS = TILE // S5_T


def _s5_prep_body(ls_ref, lrr_ref, lir_ref, lrc_ref, lic_ref, p1_ref, p2_ref, q1_ref, q2_ref,
                  g_ref, win_ref, wout_ref, ar_ref, ai_ref):
    d = pl.program_id(0)
    fwd = d == 0
    n = LANES
    step = jnp.exp(ls_ref[0, 0])
    lo = _iota((1, n), 1) < S5_STATE
    top = _iota((n, 1), 0) < S5_STATE

    lre, lim = lrr_ref[0, 0], lir_ref[0, 0]
    lr, ang = lre * step, lim * step
    er = jnp.exp(lr)
    nr, ni = er * jnp.cos(ang) - 1.0, er * jnp.sin(ang)
    den = lre * lre + lim * lim
    cr, ci = (nr * lre + ni * lim) / den, (ni * lre - nr * lim) / den
    p1, p2 = p1_ref[0], p2_ref[0]
    bbr, bbi = cr * p1 - ci * p2, cr * p2 + ci * p1
    s_idx = _iota((n, 1), 0) // S5_GROUP
    pos_s = jnp.where(fwd, s_idx, S5_T - 1 - s_idx)
    xs = (S5_T - 1 - pos_s).astype(F32)
    e = jnp.exp(xs * lr)
    l_re, l_im = e * jnp.cos(xs * ang), e * jnp.sin(xs * ang)
    w_in = jnp.where(lo, bbr * l_re - bbi * l_im, bbr * l_im + bbi * l_re)
    win_ref[0, 0] = w_in.astype(BF16)

    lrc, angc = lrc_ref[0, 0] * step, lic_ref[0, 0] * step
    q1, q2 = q1_ref[0, 0], q2_ref[0, 0]
    t_idx = _iota((1, n), 1) // S5_GROUP
    pos_t = jnp.where(fwd, t_idx, S5_T - 1 - t_idx)

    def out_op(xt):
        ex = jnp.exp(lrc * xt)
        o_re, o_im = ex * jnp.cos(angc * xt), ex * jnp.sin(angc * xt)
        return jnp.where(top, q1 * o_re - q2 * o_im, -(q1 * o_im + q2 * o_re))

    wout_ref[0, 0] = out_op((pos_t + 1).astype(F32)).astype(BF16)
    g = jnp.dot(w_in, out_op((pos_t - (S5_T - 1)).astype(F32)), precision=HIGHEST,
                preferred_element_type=F32)
    g_ref[0, 0] = jnp.where(pos_t >= pos_s, g, 0.0).astype(BF16)

    r_idx = _iota((S5_ROWS, 1), 0)
    xe = (S5_T * (jnp.where(fwd, r_idx, S5_ROWS - 1 - r_idx) + 1)).astype(F32)
    ee = jnp.exp(xe * lr)
    a_im = ee * jnp.sin(xe * ang)
    ar_ref[0, 0] = ee * jnp.cos(xe * ang)
    ai_ref[0, 0] = jnp.where(lo, -a_im, a_im)


def _s5_prep(lam_re, lam_im, log_step, b_re, b_im, c_re, c_im):
    n = LANES
    dup_l = lambda a: jnp.concatenate([a, a], axis=-1)
    dup_r = lambda a: jnp.concatenate([a, a], axis=-2)
    lrr, lir = dup_l(lam_re)[:, :, None, :], dup_l(lam_im)[:, :, None, :]
    lrc, lic = lrr.reshape(2, S5_GROUPS, n, 1), lir.reshape(2, S5_GROUPS, n, 1)
    ls = log_step.reshape(2, S5_GROUPS, 1, 1)
    tile_rows = lambda a: dup_l(jnp.tile(jnp.swapaxes(a, -1, -2), (1, S5_T, 1)))
    p1, p2 = tile_rows(b_re), tile_rows(b_im)
    tile_cols = lambda a: dup_r(jnp.tile(jnp.swapaxes(a, -1, -2), (1, 1, 1, S5_T)))
    q1, q2 = tile_cols(c_re), tile_cols(c_im)
    dg = lambda *blk: pl.BlockSpec((1, 1) + blk, lambda d, g: (d, g, 0, 0))
    gg = lambda *blk: pl.BlockSpec((1,) + blk, lambda d, g: (g, 0, 0))
    out = lambda rows, dtype: jax.ShapeDtypeStruct((2, S5_GROUPS, rows, n), dtype)
    return pl.pallas_call(
        _s5_prep_body,
        grid=(2, S5_GROUPS),
        in_specs=[dg(1, 1), dg(1, n), dg(1, n), dg(n, 1), dg(n, 1), gg(n, n), gg(n, n), dg(n, n), dg(n, n)],
        out_specs=[dg(n, n), dg(n, n), dg(n, n), dg(S5_ROWS, n), dg(S5_ROWS, n)],
        out_shape=[out(n, BF16), out(n, BF16), out(n, BF16), out(S5_ROWS, F32), out(S5_ROWS, F32)],
        compiler_params=_params(("arbitrary", "arbitrary")),
        name="s5_prep",
    )(ls, lrr, lir, lrc, lic, p1, p2, q1, q2)


def _s5_body(u_ref, g_ref, win_ref, wout_ref, ar_ref, ai_ref, y_ref, h_ref):
    d, i = pl.program_id(1), pl.program_id(2)
    fwd = d == 0

    @pl.when(i == 0)
    def _():
        h_ref[...] = jnp.zeros_like(h_ref)

    rws = S5_ROWS
    lane_t = _iota((1, LANES), 1) // S5_GROUP
    r_idx = _iota((rws, 1), 0)
    pos = jnp.where(fwd, r_idx, rws - 1 - r_idx)
    per_slab = LANES // S5_GROUP

    def travel_shift(v, k):
        return jnp.where(fwd, pltpu.roll(v, k, 0), pltpu.roll(v, rws - k, 0))

    def cmul(a_re, a_im_signed, v):
        return a_re * v + a_im_signed * pltpu.roll(v, S5_STATE, 1)

    def slab(k, carry):
        parts = [u_ref[0, k, pl.ds(t, rws, stride=S5_T), :] for t in range(S5_T)]
        ys = []
        for gi in range(per_slab):
            g = k * per_slab + gi
            x = None
            for t in range(S5_T):
                sh = ((t - gi) * S5_GROUP) % LANES
                piece = parts[t] if sh == 0 else pltpu.roll(parts[t], sh, 1)
                x = piece if x is None else jnp.where(lane_t == t, piece, x)
            xb = x.astype(BF16)
            y = _dot(xb, g_ref[0, g])
            h = _dot(xb, win_ref[0, g])
            span = 1
            while span < rws:
                idx = jnp.where(fwd, span - 1, rws - span)
                term = cmul(ar_ref[0, g, pl.ds(idx, 1), :], ai_ref[0, g, pl.ds(idx, 1), :],
                            travel_shift(h, span))
                h = h + jnp.where(pos >= span, term, 0.0)
                span *= 2
            h_in = jnp.broadcast_to(h_ref[g, 0:1, :], (rws, LANES))
            h = h + cmul(ar_ref[0, g], ai_ref[0, g], h_in)
            h_prev = jnp.where(pos == 0, h_in, travel_shift(h, 1))
            h_ref[g, 0:1, :] = jnp.where(fwd, h[rws - 1:rws, :], h[0:1, :])
            ys.append(y + _dot(h_prev.astype(BF16), wout_ref[0, g]))
        for t in range(S5_T):
            out = None
            for gi in range(per_slab):
                sh = ((gi - t) * S5_GROUP) % LANES
                piece = ys[gi] if sh == 0 else pltpu.roll(ys[gi], sh, 1)
                out = piece if out is None else jnp.where(lane_t == gi, piece, out)
            y_ref[0, 0, k, pl.ds(t, rws, stride=S5_T), :] = out
        return carry

    lax.fori_loop(0, S5_WIDTH // LANES, slab, 0)


def _s5_scan(u5, ops, *, seq):
    n_batch, slabs, total, _ = u5.shape
    n_lat_tiles = seq // TILE
    tile = lambda d, i: _tile_index(d, i, n_lat_tiles)
    op = lambda a: pl.BlockSpec((1,) + a.shape[1:], lambda b, d, i: (d, 0, 0, 0))
    return pl.pallas_call(
        _s5_body,
        grid=(n_batch, 2, n_lat_tiles + 1),
        in_specs=[pl.BlockSpec((1, slabs, TILE, LANES), lambda b, d, i: (b, 0, tile(d, i), 0))]
        + [op(a) for a in ops],
        out_specs=pl.BlockSpec((1, 1, slabs, TILE, LANES), lambda b, d, i: (d, b, 0, tile(d, i), 0)),
        out_shape=jax.ShapeDtypeStruct((2, n_batch, slabs, total, LANES), F32),
        scratch_shapes=[pltpu.VMEM((S5_GROUPS, 8, LANES), F32)],
        compiler_params=_params(("arbitrary", "arbitrary", "arbitrary")),
        name="s5_scan",
    )(u5, *ops)


def _outproj_body(*refs, col_major, last, n_batch, n_lat_tiles):
    n_tok = 1 if last else 2
    lat_ref = refs[0]
    ctx_ref = None if last else refs[1]
    (mod_ref, yf_ref, yb_ref, z_ref, of_ref, ob_ref, gg_ref, sf_ref, sb_ref, u5_ref, sg_ref,
     snw_ref, gnw_ref, s5d_ref, gluw_ref, glub_ref, wout_ref, fnw_ref, o_ref) = refs[n_tok:]
    b, j = pl.program_id(0), pl.program_id(1)
    is_ctx = j == n_lat_tiles
    res = _load_tile(lat_ref, ctx_ref, col_major, is_ctx)
    gate = mod_ref[pl.ds(jnp.where(is_ctx, n_batch, b), 1), 2 * D_MODEL:3 * D_MODEL]

    def rms(v):
        return v * lax.rsqrt(jnp.mean(v * v, axis=-1, keepdims=True) + EPS)

    y = (yf_ref[0, 0] + yb_ref[0, 0]) * _silu(z_ref[0])
    half = SSD_WIDTH // 2
    ssd = jnp.concatenate([rms(y[:, :half]), rms(y[:, half:])], axis=1) * snw_ref[...]
    out = _dot(ssd.astype(BF16), wout_ref[0:SSD_WIDTH, :])

    o = of_ref[0, 0] + ob_ref[0, 0]
    gla = jnp.concatenate([rms(o[:, h * GLA_DV:(h + 1) * GLA_DV]) for h in range(GLA_HEADS)], axis=1)
    gla = gla * gnw_ref[...] * _silu(gg_ref[0])
    out = out + _dot(gla.astype(BF16), wout_ref[SSD_WIDTH:SSD_WIDTH + GLA_WIDTH, :])

    slabs = range(S5_WIDTH // LANES)
    y5 = jnp.concatenate([sf_ref[0, 0, k] + sb_ref[0, 0, k] for k in slabs], axis=1)
    u5 = jnp.concatenate([u5_ref[0, k] for k in slabs], axis=1)
    y5 = jax.nn.gelu(y5 + s5d_ref[...] * u5)
    pr = _dot(y5.astype(BF16), gluw_ref[...]) + glub_ref[...]
    s5 = pr[:, :S5_WIDTH] * jax.nn.sigmoid(pr[:, S5_WIDTH:]) * _silu(sg_ref[0])
    out = out + _dot(s5.astype(BF16), wout_ref[SSD_WIDTH + GLA_WIDTH:, :])

    h_new = res + gate * out
    if last:
        h_new = rms(h_new) * fnw_ref[...]
    if col_major:
        rows = o_ref.shape[1]
        for k in range(o_ref.shape[2] // D_MODEL):
            o_ref[0, :, k * D_MODEL:(k + 1) * D_MODEL] = h_new[k * rows:(k + 1) * rows, :]
    else:
        o_ref[0] = h_new


def _outproj(lat, ctx, ctx_block, mod_l, y_ssd, z, o_gla, gg, y_s5, u5, sg, ssd_norm_w, gla_norm_w, s5_d,
             glu_w, glu_b, w_out, final_norm_w, *, col_major, last, n_batch, seq, rows):
    n_lat_tiles = seq // TILE
    n_tiles = n_lat_tiles if last else n_lat_tiles + 1
    tok = lambda width: pl.BlockSpec((1, TILE, width), lambda b, j: (b, j, 0))
    both = lambda width: [pl.BlockSpec((1, 1, TILE, width), lambda b, j, dd=dd: (dd, b, j, 0)) for dd in (0, 1)]
    slabs = S5_WIDTH // LANES
    s5_both = [pl.BlockSpec((1, 1, slabs, TILE, LANES), lambda b, j, dd=dd: (dd, b, 0, j, 0)) for dd in (0, 1)]
    full = lambda a: pl.BlockSpec(a.shape, lambda b, j: (0,) * a.ndim)
    small = [ssd_norm_w, gla_norm_w, s5_d, glu_w, glu_b, w_out, final_norm_w]
    tokens = (lat,) if last else (lat, ctx)
    if col_major:
        cols = TILE // rows
        out_spec = pl.BlockSpec((1, rows, cols * D_MODEL), lambda b, j: (b, 0, j))
        out_shape = jax.ShapeDtypeStruct((n_batch, rows, (seq // rows) * D_MODEL), F32)
    else:
        out_spec = tok(D_MODEL)
        out_shape = jax.ShapeDtypeStruct((n_batch, n_tiles * TILE, D_MODEL), F32)
    return pl.pallas_call(
        functools.partial(_outproj_body, col_major=col_major, last=last, n_batch=n_batch,
                          n_lat_tiles=n_lat_tiles),
        grid=(n_batch, n_tiles),
        in_specs=_token_specs(n_lat_tiles, rows, col_major, not last, ctx_block)
        + [full(mod_l)] + both(SSD_WIDTH) + [tok(SSD_WIDTH)] + both(GLA_WIDTH) + [tok(GLA_WIDTH)]
        + s5_both + [pl.BlockSpec((1, slabs, TILE, LANES), lambda b, j: (b, 0, j, 0)), tok(S5_WIDTH)]
        + [full(a) for a in small],
        out_specs=out_spec,
        out_shape=out_shape,
        compiler_params=_params(("arbitrary", "arbitrary")),
        name="outproj_col" if col_major else "outproj_row",
    )(*tokens, mod_l, y_ssd, y_ssd, z, o_gla, o_gla, gg, y_s5, y_s5, u5, sg, *small)


def _pack_w_in(w):
    sizes = (1024, SSD_CONV_DIM, 32, 256, 256, 512, 512, 32, 512, 512)
    parts, start = [], 0
    for size in sizes:
        parts.append(w[:, start:start + size])
        start += size
    z, xbc, dt, q, k, v, gg, lr, u5, sg = parts
    pad = lambda a: jnp.pad(a, ((0, 0), (0, LANES - a.shape[1])))
    cols = [z, xbc, pad(dt[:, :16]), pad(dt[:, 16:]), q, k, v, gg, pad(lr[:, :16]), pad(lr[:, 16:]), u5, sg]
    return jnp.concatenate(cols, axis=1).astype(BF16)


def kernel(x, c, ctx, c_ctx, norm_w, mod_w, mod_b, w_in, w_out, ssd_conv_w, ssd_conv_b, ssd_a_log, ssd_dt_bias, ssd_d, ssd_norm_w, gla_w_lr, gla_b_lr, gla_norm_w, s5_lam_re, s5_lam_im, s5_log_step, s5_b_re, s5_b_im, s5_c_re, s5_c_im, s5_d, s5_glu_w, s5_glu_b, final_norm_w):
    n_batch, seq, d_model = x.shape
    depth = norm_w.shape[0]
    rows = seq // GRID_W
    assert d_model == D_MODEL and ctx.shape[1] == CTX_LEN == TILE
    assert seq % TILE == 0 and TILE % rows == 0
    assert depth == 2, "layer 0 row-major with context update, layer 1 column-major and last"
    n_lat_tiles = seq // TILE

    mod_rows = -(-(n_batch + 1) // 8) * 8
    cc = jnp.zeros((mod_rows, d_model), F32).at[:n_batch].set(c).at[n_batch].set(c_ctx)
    mod = _modulation(cc, mod_w, mod_b)

    lat, ctx_arr, ctx_block = x, ctx, 0
    for l in range(depth):
        col_major = l % 2 == 1
        last = l == depth - 1
        if col_major:
            lat_in = lat.reshape(n_batch, lat.shape[1] // GRID_W, GRID_W * d_model)
        else:
            lat_in = lat
        z, xbc, dt, qkv, gg, lr, u5, sg = _inproj(
            lat_in, ctx_arr, ctx_block, mod[l], norm_w[l].reshape(1, -1), _pack_w_in(w_in[l]),
            col_major=col_major, n_batch=n_batch, seq=seq, rows=rows)

        xc = _conv(xbc, ssd_conv_w[l], ssd_conv_b[l], seq=seq)
        pad_h = lambda a: jnp.pad(a, ((0, 0), (0, LANES - a.shape[1])))
        y_ssd = _ssd_scan(xc, dt, pad_h(ssd_dt_bias[l]), pad_h(ssd_a_log[l]),
                          jnp.repeat(ssd_d[l], SSD_HEADDIM).reshape(1, -1), seq=seq)

        w_lr = jnp.pad(gla_w_lr[l], ((0, 0), (0, LANES - GLA_RANK), (0, 0)))
        o_gla = _gla_scan(qkv, lr, w_lr, gla_b_lr[l].reshape(2, 1, -1), seq=seq)

        ops = _s5_prep(s5_lam_re[l], s5_lam_im[l], s5_log_step[l], s5_b_re[l], s5_b_im[l],
                       s5_c_re[l], s5_c_im[l])
        y_s5 = _s5_scan(u5, ops, seq=seq)

        out = _outproj(
            lat_in, ctx_arr, ctx_block, mod[l], y_ssd, z, o_gla, gg, y_s5, u5, sg,
            ssd_norm_w[l].reshape(1, -1), jnp.tile(gla_norm_w[l], GLA_HEADS).reshape(1, -1),
            s5_d[l].reshape(1, -1), s5_glu_w[l].astype(BF16), s5_glu_b[l].reshape(1, -1),
            w_out[l].astype(BF16), final_norm_w.reshape(1, -1),
            col_major=col_major, last=last, n_batch=n_batch, seq=seq, rows=rows)
        if last:
            return out.reshape(n_batch, seq, d_model)
        lat, ctx_arr, ctx_block = out, out, n_lat_tiles
```

```python
import functools

import jax
import jax.numpy as jnp
from jax import lax
from jax.experimental import pallas as pl
from jax.experimental.pallas import tpu as pltpu

F32, BF16 = jnp.float32, jnp.bfloat16
HIGHEST = lax.Precision.HIGHEST
EPS = 1e-6

D_MODEL = 1024
GRID_W = 64
CTX_LEN = 256
TILE = 256
LANES = 128
VMEM_LIMIT = 56 * 1024 * 1024

SSD_WIDTH, SSD_HEADS, SSD_HEADDIM, SSD_STATE, SSD_CHUNK = 1024, 16, 64, 128, 128
SSD_CONV_DIM, SSD_CONV_K = 1536, 5
GLA_HEADS, GLA_DK, GLA_DV, GLA_RANK, GLA_TAU, GLA_CHUNK = 4, 64, 128, 16, 16.0, 64
GLA_WIDTH = GLA_HEADS * GLA_DV
S5_WIDTH, S5_GROUP, S5_GROUPS, S5_STATE = 512, 16, 32, 64
S5_T = LANES // S5_GROUP
MIX_WIDTH = SSD_WIDTH + GLA_WIDTH + S5_WIDTH

C_Z, C_XBC, C_DT = 0, 1024, 2560
C_QKV, C_GG, C_LR = 2816, 3840, 4352
C_U5, C_SG, C_END = 4608, 5120, 5632


def _params(semantics):
    return pltpu.CompilerParams(dimension_semantics=semantics, vmem_limit_bytes=VMEM_LIMIT)


def _silu(v):
    return v * jax.nn.sigmoid(v)


def _dot(a, b):
    return jnp.dot(a, b, preferred_element_type=F32)


def _dot_nt(a, b):
    return lax.dot_general(a, b, (((1,), (1,)), ((), ())), preferred_element_type=F32)


def _dot_exact_lhs(m_bf16, v):
    hi = v.astype(BF16)
    r1 = v - hi.astype(F32)
    mid = r1.astype(BF16)
    lo = (r1 - mid.astype(F32)).astype(BF16)
    return _dot(m_bf16, hi) + _dot(m_bf16, mid) + _dot(m_bf16, lo)


def _iota(shape, dim):
    return lax.broadcasted_iota(jnp.int32, shape, dim)


def _interleave(*stages):
    active = list(stages)
    while active:
        for gen in list(active):
            try:
                next(gen)
            except StopIteration:
                active.remove(gen)


def _mod_body(cc_ref, w_ref, b_ref, o_ref):
    s = _silu(cc_ref[...])
    o_ref[0] = jnp.dot(s, w_ref[0], precision=HIGHEST, preferred_element_type=F32) + b_ref[0]


def _modulation(cc, mod_w, mod_b):
    depth, d, three_d = mod_w.shape
    rows = cc.shape[0]
    return pl.pallas_call(
        _mod_body,
        grid=(depth, three_d // d),
        in_specs=[pl.BlockSpec((rows, d), lambda l, n: (0, 0)),
                  pl.BlockSpec((1, d, d), lambda l, n: (l, 0, n)),
                  pl.BlockSpec((1, 1, d), lambda l, n: (l, 0, n))],
        out_specs=pl.BlockSpec((1, rows, d), lambda l, n: (l, 0, n)),
        out_shape=jax.ShapeDtypeStruct((depth, rows, three_d), F32),
        compiler_params=_params(("arbitrary", "arbitrary")),
        name="modulation",
    )(cc, mod_w, mod_b.reshape(depth, 1, three_d))


def _load_tile(lat_ref, ctx_ref, col_major, is_ctx):
    if col_major:
        lat = jnp.concatenate(
            [lat_ref[0, :, k * D_MODEL:(k + 1) * D_MODEL] for k in range(lat_ref.shape[2] // D_MODEL)], axis=0)
    else:
        lat = lat_ref[0]
    if ctx_ref is None:
        return lat
    return jnp.where(is_ctx, ctx_ref[0], lat)


def _token_specs(n_lat_tiles, rows, col_major, with_ctx, ctx_block):
    last = n_lat_tiles - 1
    if col_major:
        cols = TILE // rows
        lat = pl.BlockSpec((1, rows, cols * D_MODEL), lambda b, j: (b, 0, jnp.minimum(j, last)))
    else:
        lat = pl.BlockSpec((1, TILE, D_MODEL), lambda b, j: (b, jnp.minimum(j, last), 0))
    if not with_ctx:
        return [lat]
    return [lat, pl.BlockSpec((1, TILE, D_MODEL), lambda b, j: (b, ctx_block, 0))]


def _inproj_body(*refs, col_major, n_batch, n_lat_tiles):
    lat_ref, ctx_ref, mod_ref, nw_ref, w_ref = refs[:5]
    z_ref, xbc_ref, dt_ref, qkv_ref, gg_ref, lr_ref, u5_ref, sg_ref = refs[5:]
    b, j = pl.program_id(0), pl.program_id(1)
    is_ctx = j == n_lat_tiles
    x = _load_tile(lat_ref, ctx_ref, col_major, is_ctx)
    m = mod_ref[pl.ds(jnp.where(is_ctx, n_batch, b), 1), :]
    shift, scale = m[:, :D_MODEL], m[:, D_MODEL:2 * D_MODEL]
    u = x * lax.rsqrt(jnp.mean(x * x, axis=-1, keepdims=True) + EPS) * nw_ref[...]
    u = (u * (1.0 + scale) + shift).astype(BF16)
    z_ref[0] = _dot(u, w_ref[:, C_Z:C_XBC])
    xbc_ref[0] = _dot(u, w_ref[:, C_XBC:C_DT])
    dt_ref[0] = _dot(u, w_ref[:, C_DT:C_QKV])
    qkv_ref[0] = _dot(u, w_ref[:, C_QKV:C_GG])
    gg_ref[0] = _dot(u, w_ref[:, C_GG:C_LR])
    lr_ref[0] = _dot(u, w_ref[:, C_LR:C_U5])
    u5 = _dot(u, w_ref[:, C_U5:C_SG])
    for k in range(S5_WIDTH // LANES):
        u5_ref[0, k] = u5[:, k * LANES:(k + 1) * LANES]
    sg_ref[0] = _dot(u, w_ref[:, C_SG:C_END])


def _inproj(lat, ctx, ctx_block, mod_l, norm_w, w_packed, *, col_major, n_batch, seq, rows):
    n_lat_tiles = seq // TILE
    lt = seq + CTX_LEN
    tok = lambda width: pl.BlockSpec((1, TILE, width), lambda b, j: (b, j, 0))
    full = lambda a: pl.BlockSpec(a.shape, lambda b, j: (0,) * a.ndim)
    shape = lambda width: jax.ShapeDtypeStruct((n_batch, lt, width), F32)
    return pl.pallas_call(
        functools.partial(_inproj_body, col_major=col_major, n_batch=n_batch, n_lat_tiles=n_lat_tiles),
        grid=(n_batch, n_lat_tiles + 1),
        in_specs=_token_specs(n_lat_tiles, rows, col_major, True, ctx_block)
        + [full(mod_l), full(norm_w), full(w_packed)],
        out_specs=[tok(1024), tok(SSD_CONV_DIM), tok(256), tok(1024), tok(512), tok(256),
                   pl.BlockSpec((1, S5_WIDTH // LANES, TILE, LANES), lambda b, j: (b, 0, j, 0)),
                   tok(512)],
        out_shape=[shape(1024), shape(SSD_CONV_DIM), shape(256), shape(1024), shape(512), shape(256),
                   jax.ShapeDtypeStruct((n_batch, S5_WIDTH // LANES, lt, LANES), F32), shape(512)],
        compiler_params=_params(("arbitrary", "arbitrary")),
        name="inproj_col" if col_major else "inproj_row",
    )(lat, ctx, mod_l, norm_w, w_packed)


CONV_ROWS = 256
CONV_COLS = 256
HALO = 8


def _conv_body(x_ref, w_ref, b_ref, o_ref, *, seq, total):
    n_chunks = total // CONV_ROWS
    lat_chunks = seq // CONV_ROWS
    w = w_ref[...]
    bias = b_ref[...]

    def chunk(i, carry):
        r0 = pl.multiple_of(i * CONV_ROWS, CONV_ROWS)
        main = x_ref[0, pl.ds(r0, CONV_ROWS), :]
        at_start = (i == 0) | (i == lat_chunks)
        at_end = (i == lat_chunks - 1) | (i == n_chunks - 1)
        p0 = pl.multiple_of(jnp.maximum(r0 - HALO, 0), HALO)
        n0 = pl.multiple_of(jnp.minimum(r0 + CONV_ROWS, total - HALO), HALO)
        prev = jnp.where(at_start, 0.0, x_ref[0, pl.ds(p0, HALO), :])
        nxt = jnp.where(at_end, 0.0, x_ref[0, pl.ds(n0, HALO), :])
        ext = jnp.concatenate([prev, main, nxt], axis=0)
        acc = jnp.broadcast_to(bias, main.shape)
        for k in range(SSD_CONV_K):
            off = HALO + k - SSD_CONV_K // 2
            acc = acc + w[k:k + 1, :] * ext[off:off + CONV_ROWS, :]
        o_ref[0, pl.ds(r0, CONV_ROWS), :] = _silu(acc)
        return carry

    lax.fori_loop(0, n_chunks, chunk, 0)


def _conv(xbc, conv_w, conv_b, *, seq):
    n_batch, total, ch = xbc.shape
    return pl.pallas_call(
        functools.partial(_conv_body, seq=seq, total=total),
        grid=(n_batch, ch // CONV_COLS),
        in_specs=[pl.BlockSpec((1, total, CONV_COLS), lambda b, c: (b, 0, c)),
                  pl.BlockSpec((SSD_CONV_K, CONV_COLS), lambda b, c: (0, c)),
                  pl.BlockSpec((1, CONV_COLS), lambda b, c: (0, c))],
        out_specs=pl.BlockSpec((1, total, CONV_COLS), lambda b, c: (b, 0, c)),
        out_shape=jax.ShapeDtypeStruct(xbc.shape, F32),
        compiler_params=_params(("arbitrary", "arbitrary")),
        name="ssd_conv",
    )(xbc, conv_w, conv_b.reshape(1, ch))


def _ssd_body(xcf_ref, dtf_ref, xcb_ref, dtb_ref, bias_ref, alog_ref, dsk_ref, yf_ref, yb_ref, stf_ref, stb_ref):
    @pl.when(pl.program_id(1) == 0)
    def _():
        stf_ref[...] = jnp.zeros_like(stf_ref)
        stb_ref[...] = jnp.zeros_like(stb_ref)

    _interleave(_ssd_dir(xcf_ref, dtf_ref, bias_ref, alog_ref, dsk_ref, yf_ref, stf_ref, rev=False),
                _ssd_dir(xcb_ref, dtb_ref, bias_ref, alog_ref, dsk_ref, yb_ref, stb_ref, rev=True))


def _ssd_dir(xc_ref, dt_ref, bias_ref, alog_ref, dsk_ref, y_ref, st_ref, *, rev):
    t = SSD_CHUNK
    d = 1 if rev else 0
    dt = jax.nn.softplus(dt_ref[0] + bias_ref[d:d + 1, :])
    da = dt * -jnp.exp(alog_ref[d:d + 1, :])
    row, col = _iota((t, t), 0), _iota((t, t), 1)
    tri = row <= col if rev else row >= col
    p = _dot_exact_lhs(jnp.where(tri, 1.0, 0.0).astype(BF16), da)
    p_t = p.T
    total = jnp.sum(da, axis=0, keepdims=True)
    e_tot = jnp.exp(total)
    lo = _iota((1, LANES), 1) < SSD_HEADDIM
    hi = jnp.logical_not(lo)
    pairs_per_group = SSD_HEADS // 2 // 2
    yield
    for g in range(2):
        bg = xc_ref[0, :, SSD_WIDTH + g * SSD_STATE:SSD_WIDTH + (g + 1) * SSD_STATE]
        cg = xc_ref[0, :, SSD_WIDTH + (2 + g) * SSD_STATE:SSD_WIDTH + (3 + g) * SSD_STATE]
        cg_b = cg.astype(BF16)
        cb = _dot_nt(cg_b, bg.astype(BF16))
        bg_t = bg.T.astype(BF16)
        for q in range(pairs_per_group):
            hp = g * pairs_per_group + q
            h0, h1 = 2 * hp, 2 * hp + 1
            pb0 = jnp.broadcast_to(p[:, h0:h0 + 1], (t, LANES))
            pb1 = jnp.broadcast_to(p[:, h1:h1 + 1], (t, LANES))
            pbp = jnp.where(lo, pb0, pb1)
            tot = jnp.where(lo, total[:, h0:h0 + 1], total[:, h1:h1 + 1])
            xp = xc_ref[0, :, hp * LANES:(hp + 1) * LANES]
            xdt = xp * jnp.where(lo, dt[:, h0:h0 + 1], dt[:, h1:h1 + 1])
            st = st_ref[hp]
            acc = jnp.exp(pbp) * _dot(cg_b, st.astype(BF16))
            if not rev:
                acc = acc + dsk_ref[:, hp * LANES:(hp + 1) * LANES] * xp
            for h, pb, sel in ((h0, pb0, lo), (h1, pb1, hi)):
                decay = jnp.where(tri, jnp.exp(pb - p_t[h:h + 1, :]), 0.0)
                acc = acc + _dot((cb * decay).astype(BF16), jnp.where(sel, xdt, 0.0).astype(BF16))
            y_ref[0, :, hp * LANES:(hp + 1) * LANES] = acc
            dec = jnp.where(lo, e_tot[:, h0:h0 + 1], e_tot[:, h1:h1 + 1])
            st_ref[hp] = dec * st + _dot(bg_t, (xdt * jnp.exp(tot - pbp)).astype(BF16))
            yield


def _ssd_scan(xc, dt, dt_bias, a_log, dsk, *, seq):
    n_batch, total, _ = xc.shape
    nc = total // SSD_CHUNK
    lat_c = seq // SSD_CHUNK
    ctx_c = nc - lat_c
    fwd = lambda i: jnp.where(i < ctx_c, lat_c + i, i - ctx_c)
    bwd = lambda i: nc - 1 - i
    full = lambda a: pl.BlockSpec(a.shape, lambda b, i: (0,) * a.ndim)
    xc_spec = lambda chunk: pl.BlockSpec((1, SSD_CHUNK, SSD_CONV_DIM), lambda b, i: (b, chunk(i), 0))
    dt_spec = lambda chunk, d: pl.BlockSpec((1, SSD_CHUNK, LANES), lambda b, i: (b, chunk(i), d))
    y_spec = lambda chunk: pl.BlockSpec((1, SSD_CHUNK, SSD_WIDTH), lambda b, i: (b, chunk(i), 0))
    y_shape = jax.ShapeDtypeStruct((n_batch, total, SSD_WIDTH), F32)
    return pl.pallas_call(
        _ssd_body,
        grid=(n_batch, nc),
        in_specs=[xc_spec(fwd), dt_spec(fwd, 0), xc_spec(bwd), dt_spec(bwd, 1),
                  full(dt_bias), full(a_log), full(dsk)],
        out_specs=[y_spec(fwd), y_spec(bwd)],
        out_shape=[y_shape, y_shape],
        scratch_shapes=[pltpu.VMEM((SSD_HEADS // 2, SSD_STATE, LANES), F32)] * 2,
        compiler_params=_params(("arbitrary", "arbitrary")),
        name="ssd_scan",
    )(xc, dt, xc, dt, dt_bias, a_log, dsk)


def _tile_index(d, i, n_lat_tiles):
    return jnp.where(i == 0, n_lat_tiles, jnp.where(d == 0, i - 1, n_lat_tiles - i))


def _gla_body(qkvf_ref, lrf_ref, qkvb_ref, lrb_ref, wlr_ref, blr_ref, of_ref, ob_ref, sf_ref, sb_ref):
    @pl.when(pl.program_id(1) == 0)
    def _():
        sf_ref[...] = jnp.zeros_like(sf_ref)
        sb_ref[...] = jnp.zeros_like(sb_ref)

    _interleave(_gla_dir(qkvf_ref, lrf_ref, wlr_ref, blr_ref, of_ref, sf_ref, rev=False),
                _gla_dir(qkvb_ref, lrb_ref, wlr_ref, blr_ref, ob_ref, sb_ref, rev=True))


def _gla_dir(qkv_ref, lr_ref, wlr_ref, blr_ref, o_ref, s_ref, *, rev):
    d = 1 if rev else 0
    t = GLA_CHUNK
    n_chunks = TILE // t
    kw = GLA_HEADS * GLA_DK
    row, col = _iota((TILE, TILE), 0), _iota((TILE, TILE), 1)
    same = row // t == col // t
    tri = jnp.where(same, row - col, 1 if rev else -1)
    tri = tri <= 0 if rev else tri >= 0
    lo = _iota((1, LANES), 1) < GLA_DK
    hi = jnp.logical_not(lo)
    eye = _iota((LANES, LANES), 0) == _iota((LANES, LANES), 1)
    row_lo = _iota((LANES, 1), 0) < GLA_DK

    logit = jnp.dot(lr_ref[0], wlr_ref[d], precision=HIGHEST, preferred_element_type=F32) + blr_ref[d]
    yield
    g = jax.nn.log_sigmoid(logit) / GLA_TAU
    p = _dot_exact_lhs(jnp.where(tri, 1.0, 0.0).astype(BF16), g)
    total = _dot_exact_lhs(jnp.where(same, 1.0, 0.0).astype(BF16), g)
    yield
    q = qkv_ref[0, :, 0:kw] * GLA_DK ** -0.5
    k = qkv_ref[0, :, kw:2 * kw]
    qe = q * jnp.exp(p)
    ke = k * jnp.exp(-p)
    kd = k * jnp.exp(total - p)
    e_tot = jnp.exp(total)
    order = range(n_chunks - 1, -1, -1) if rev else range(n_chunks)
    pairs = range(GLA_HEADS // 2)
    lanes = lambda pr: slice(pr * LANES, (pr + 1) * LANES)
    qms, intra, vs = {}, {}, {}
    for pr in pairs:
        ke_p = ke[:, lanes(pr)].astype(BF16)
        for e, sel in ((0, lo), (1, hi)):
            h = 2 * pr + e
            qms[h] = jnp.where(sel, qe[:, lanes(pr)], 0.0).astype(BF16)
            vs[h] = qkv_ref[0, :, 2 * kw + h * GLA_DV:2 * kw + (h + 1) * GLA_DV].astype(BF16)
            attn = jnp.where(tri, _dot_nt(qms[h], ke_p), 0.0).astype(BF16)
            yield
            intra[h] = _dot(attn, vs[h])
            yield
    upd, dec = {}, {}
    for pr in pairs:
        for c in order:
            rows = slice(c * t, (c + 1) * t)
            kd_t = kd[rows, lanes(pr)].T.astype(BF16)
            upd[pr, c] = jnp.where(row_lo, _dot(kd_t, vs[2 * pr][rows]), _dot(kd_t, vs[2 * pr + 1][rows]))
            dec[pr, c] = jnp.sum(jnp.where(eye, e_tot[c * t:c * t + 1, lanes(pr)], 0.0), axis=1, keepdims=True)
        yield
    for pr in pairs:
        s = s_ref[pr]
        for c in order:
            rows = slice(c * t, (c + 1) * t)
            s_b = s.astype(BF16)
            for h in (2 * pr, 2 * pr + 1):
                o_ref[0, rows, h * GLA_DV:(h + 1) * GLA_DV] = intra[h][rows] + _dot(qms[h][rows], s_b)
            s = dec[pr, c] * s + upd[pr, c]
            yield
        s_ref[pr] = s


def _gla_scan(qkv, lr, w_lr, b_lr, *, seq):
    n_batch, total, _ = qkv.shape
    n_lat_tiles = seq // TILE
    tile = lambda d: (lambda i: _tile_index(d, i, n_lat_tiles))
    full = lambda a: pl.BlockSpec(a.shape, lambda b, i: (0,) * a.ndim)
    qkv_spec = lambda d: pl.BlockSpec((1, TILE, 1024), lambda b, i: (b, tile(d)(i), 0))
    lr_spec = lambda d: pl.BlockSpec((1, TILE, LANES), lambda b, i: (b, tile(d)(i), d))
    o_spec = lambda d: pl.BlockSpec((1, TILE, GLA_WIDTH), lambda b, i: (b, tile(d)(i), 0))
    o_shape = jax.ShapeDtypeStruct((n_batch, total, GLA_WIDTH), F32)
    return pl.pallas_call(
        _gla_body,
        grid=(n_batch, n_lat_tiles + 1),
        in_specs=[qkv_spec(0), lr_spec(0), qkv_spec(1), lr_spec(1), full(w_lr), full(b_lr)],
        out_specs=[o_spec(0), o_spec(1)],
        out_shape=[o_shape, o_shape],
        scratch_shapes=[pltpu.VMEM((GLA_HEADS // 2, LANES, GLA_DV), F32)] * 2,
        compiler_params=_params(("arbitrary", "arbitrary")),
        name="gla_scan",
    )(qkv, lr, qkv, lr, w_lr, b_lr)


S5_ROWS = TILE // S5_T


def _s5_prep_body(ls_ref, lrr_ref, lir_ref, lrc_ref, lic_ref, p1_ref, p2_ref, q1_ref, q2_ref,
                  g_ref, win_ref, wout_ref, ar_ref, ai_ref):
    d = pl.program_id(0)
    fwd = d == 0
    n = LANES
    step = jnp.exp(ls_ref[0, 0])
    lo = _iota((1, n), 1) < S5_STATE
    top = _iota((n, 1), 0) < S5_STATE

    lre, lim = lrr_ref[0, 0], lir_ref[0, 0]
    lr, ang = lre * step, lim * step
    er = jnp.exp(lr)
    nr, ni = er * jnp.cos(ang) - 1.0, er * jnp.sin(ang)
    den = lre * lre + lim * lim
    cr, ci = (nr * lre + ni * lim) / den, (ni * lre - nr * lim) / den
    p1, p2 = p1_ref[0], p2_ref[0]
    bbr, bbi = cr * p1 - ci * p2, cr * p2 + ci * p1
    s_idx = _iota((n, 1), 0) // S5_GROUP
    pos_s = jnp.where(fwd, s_idx, S5_T - 1 - s_idx)
    xs = (S5_T - 1 - pos_s).astype(F32)
    e = jnp.exp(xs * lr)
    l_re, l_im = e * jnp.cos(xs * ang), e * jnp.sin(xs * ang)
    w_in = jnp.where(lo, bbr * l_re - bbi * l_im, bbr * l_im + bbi * l_re)
    win_ref[0, 0] = w_in.astype(BF16)

    lrc, angc = lrc_ref[0, 0] * step, lic_ref[0, 0] * step
    q1, q2 = q1_ref[0, 0], q2_ref[0, 0]
    t_idx = _iota((1, n), 1) // S5_GROUP
    pos_t = jnp.where(fwd, t_idx, S5_T - 1 - t_idx)

    def out_op(xt):
        ex = jnp.exp(lrc * xt)
        o_re, o_im = ex * jnp.cos(angc * xt), ex * jnp.sin(angc * xt)
        return jnp.where(top, q1 * o_re - q2 * o_im, -(q1 * o_im + q2 * o_re))

    wout_ref[0, 0] = out_op((pos_t + 1).astype(F32)).astype(BF16)
    g = jnp.dot(w_in, out_op((pos_t - (S5_T - 1)).astype(F32)), precision=HIGHEST,
                preferred_element_type=F32)
    g_ref[0, 0] = jnp.where(pos_t >= pos_s, g, 0.0).astype(BF16)

    r_idx = _iota((S5_ROWS, 1), 0)
    xe = (S5_T * (jnp.where(fwd, r_idx, S5_ROWS - 1 - r_idx) + 1)).astype(F32)
    ee = jnp.exp(xe * lr)
    a_im = ee * jnp.sin(xe * ang)
    ar_ref[0, 0] = ee * jnp.cos(xe * ang)
    ai_ref[0, 0] = jnp.where(lo, -a_im, a_im)


def _s5_prep(lam_re, lam_im, log_step, b_re, b_im, c_re, c_im):
    n = LANES
    dup_l = lambda a: jnp.concatenate([a, a], axis=-1)
    dup_r = lambda a: jnp.concatenate([a, a], axis=-2)
    lrr, lir = dup_l(lam_re)[:, :, None, :], dup_l(lam_im)[:, :, None, :]
    lrc, lic = lrr.reshape(2, S5_GROUPS, n, 1), lir.reshape(2, S5_GROUPS, n, 1)
    ls = log_step.reshape(2, S5_GROUPS, 1, 1)
    tile_rows = lambda a: dup_l(jnp.tile(jnp.swapaxes(a, -1, -2), (1, S5_T, 1)))
    p1, p2 = tile_rows(b_re), tile_rows(b_im)
    tile_cols = lambda a: dup_r(jnp.tile(jnp.swapaxes(a, -1, -2), (1, 1, 1, S5_T)))
    q1, q2 = tile_cols(c_re), tile_cols(c_im)
    dg = lambda *blk: pl.BlockSpec((1, 1) + blk, lambda d, g: (d, g, 0, 0))
    gg = lambda *blk: pl.BlockSpec((1,) + blk, lambda d, g: (g, 0, 0))
    out = lambda rows, dtype: jax.ShapeDtypeStruct((2, S5_GROUPS, rows, n), dtype)
    return pl.pallas_call(
        _s5_prep_body,
        grid=(2, S5_GROUPS),
        in_specs=[dg(1, 1), dg(1, n), dg(1, n), dg(n, 1), dg(n, 1), gg(n, n), gg(n, n), dg(n, n), dg(n, n)],
        out_specs=[dg(n, n), dg(n, n), dg(n, n), dg(S5_ROWS, n), dg(S5_ROWS, n)],
        out_shape=[out(n, BF16), out(n, BF16), out(n, BF16), out(S5_ROWS, F32), out(S5_ROWS, F32)],
        compiler_params=_params(("arbitrary", "arbitrary")),
        name="s5_prep",
    )(ls, lrr, lir, lrc, lic, p1, p2, q1, q2)


def _s5_pair_ops(g, win, wout, ar, ai):
    hs = S5_STATE
    cat = jnp.concatenate
    pair = lambda a: (a[:, 0::2], a[:, 1::2])
    g0, g1 = pair(g)
    i0, i1 = pair(win)
    o0, o1 = pair(wout)
    z = jnp.zeros_like(g0)
    zh, zr = z[..., :hs], z[..., :hs, :]
    wa = cat([cat([g0, z, i0[..., :hs], zh, i0[..., hs:], zh], -1),
              cat([z, g1, zh, i1[..., :hs], zh, i1[..., hs:]], -1)], -2)
    wo = cat([cat([o0[..., :hs, :], zr], -1), cat([zr, o1[..., :hs, :]], -1),
              cat([o0[..., hs:, :], zr], -1), cat([zr, o1[..., hs:, :]], -1)], -2)
    a0, a1 = pair(ar)
    b0, b1 = pair(ai)
    return wa, wo, cat([a0[..., :hs], a1[..., :hs]], -1), cat([b0[..., hs:], b1[..., hs:]], -1)


def _s5_body(uf_ref, ub_ref, wa_ref, wo_ref, are_ref, aim_ref, yf_ref, yb_ref, *h_refs):
    @pl.when(pl.program_id(1) == 0)
    def _():
        for h_ref in h_refs:
            h_ref[...] = jnp.zeros_like(h_ref)

    ops = (wa_ref, wo_ref, are_ref, aim_ref)
    slab_f = _s5_dir(uf_ref, *ops, yf_ref, *h_refs[:2], rev=False)
    slab_b = _s5_dir(ub_ref, *ops, yb_ref, *h_refs[2:], rev=True)

    def slab(k, carry):
        _interleave(slab_f(k), slab_b(k))
        return carry

    lax.fori_loop(0, S5_WIDTH // LANES, slab, 0)


def _s5_dir(u_ref, wa_ref, wo_ref, are_ref, aim_ref, y_ref, hre_ref, him_ref, *, rev):
    d = 1 if rev else 0
    rws = S5_ROWS
    lane_t = _iota((1, LANES), 1) // S5_GROUP
    row = _iota((rws, 1), 0)
    pos = rws - 1 - row if rev else row
    last_row = 0 if rev else rws - 1
    per_slab = LANES // S5_GROUP
    pairs = per_slab // 2

    def earlier(v, k):
        return pltpu.roll(v, rws - k if rev else k, 0)

    def slab(k):
        parts = [u_ref[0, k, pl.ds(t, rws, stride=S5_T), :] for t in range(S5_T)]
        h_all_re, h_all_im = hre_ref[k], him_ref[k]
        xs = []
        for gi in range(per_slab):
            x = None
            for t in range(S5_T):
                sh = ((t - gi) * S5_GROUP) % LANES
                piece = parts[t] if sh == 0 else pltpu.roll(parts[t], sh, 1)
                x = piece if x is None else jnp.where(lane_t == t, piece, x)
            xs.append(x)
        yield
        first = []
        for q in range(pairs):
            x2 = jnp.concatenate([xs[2 * q], xs[2 * q + 1]], axis=1).astype(BF16)
            first.append(_dot(x2, wa_ref[d, k * pairs + q]))
            yield
        ys, new_re, new_im = [], [], []
        for q in range(pairs):
            pr = k * pairs + q
            res = first[q]
            y_intra, h_re, h_im = res[:, :2 * LANES], res[:, 2 * LANES:3 * LANES], res[:, 3 * LANES:]
            span = 1
            while span < rws:
                idx = rws - span if rev else span - 1
                a_re, a_im = are_ref[d, pr, idx:idx + 1, :], aim_ref[d, pr, idx:idx + 1, :]
                s_re = jnp.where(pos >= span, earlier(h_re, span), 0.0)
                s_im = jnp.where(pos >= span, earlier(h_im, span), 0.0)
                h_re, h_im = h_re + (a_re * s_re - a_im * s_im), h_im + (a_re * s_im + a_im * s_re)
                span *= 2
            in_re = jnp.broadcast_to(h_all_re[8 * q:8 * q + 1, :], (rws, LANES))
            in_im = jnp.broadcast_to(h_all_im[8 * q:8 * q + 1, :], (rws, LANES))
            t_re, t_im = are_ref[d, pr], aim_ref[d, pr]
            h_re, h_im = h_re + (t_re * in_re - t_im * in_im), h_im + (t_re * in_im + t_im * in_re)
            p_re = jnp.where(pos == 0, in_re, earlier(h_re, 1))
            p_im = jnp.where(pos == 0, in_im, earlier(h_im, 1))
            new_re.append(jnp.broadcast_to(h_re[last_row:last_row + 1, :], (8, LANES)))
            new_im.append(jnp.broadcast_to(h_im[last_row:last_row + 1, :], (8, LANES)))
            y2 = y_intra + _dot(jnp.concatenate([p_re, p_im], axis=1).astype(BF16), wo_ref[d, pr])
            ys += [y2[:, :LANES], y2[:, LANES:]]
            yield
        hre_ref[k] = jnp.concatenate(new_re, axis=0)
        him_ref[k] = jnp.concatenate(new_im, axis=0)
        for t in range(S5_T):
            out = None
            for gi in range(per_slab):
                sh = ((gi - t) * S5_GROUP) % LANES
                piece = ys[gi] if sh == 0 else pltpu.roll(ys[gi], sh, 1)
                out = piece if out is None else jnp.where(lane_t == gi, piece, out)
            y_ref[0, k, pl.ds(t, rws, stride=S5_T), :] = out

    return slab


def _s5_scan(u5, ops, *, seq):
    n_batch, slabs, total, _ = u5.shape
    n_lat_tiles = seq // TILE
    tok = lambda d: pl.BlockSpec((1, slabs, TILE, LANES), lambda b, i: (b, 0, _tile_index(d, i, n_lat_tiles), 0))
    full = lambda a: pl.BlockSpec(a.shape, lambda b, i: (0,) * a.ndim)
    pairs_per_slab = LANES // S5_GROUP // 2
    y_shape = jax.ShapeDtypeStruct((n_batch, slabs, total, LANES), F32)
    return pl.pallas_call(
        _s5_body,
        grid=(n_batch, n_lat_tiles + 1),
        in_specs=[tok(0), tok(1)] + [full(a) for a in ops],
        out_specs=[tok(0), tok(1)],
        out_shape=[y_shape, y_shape],
        scratch_shapes=[pltpu.VMEM((slabs, 8 * pairs_per_slab, LANES), F32)] * 4,
        compiler_params=_params(("arbitrary", "arbitrary")),
        name="s5_scan",
    )(u5, u5, *ops)


def _outproj_body(*refs, col_major, last, n_batch, n_lat_tiles):
    n_tok = 1 if last else 2
    lat_ref = refs[0]
    ctx_ref = None if last else refs[1]
    (mod_ref, yf_ref, yb_ref, z_ref, of_ref, ob_ref, gg_ref, sf_ref, sb_ref, u5_ref, sg_ref,
     snw_ref, gnw_ref, s5d_ref, gluw_ref, glub_ref, wout_ref, fnw_ref, o_ref) = refs[n_tok:]
    b, j = pl.program_id(0), pl.program_id(1)
    is_ctx = j == n_lat_tiles
    res = _load_tile(lat_ref, ctx_ref, col_major, is_ctx)
    gate = mod_ref[pl.ds(jnp.where(is_ctx, n_batch, b), 1), 2 * D_MODEL:3 * D_MODEL]

    def rms(v):
        return v * lax.rsqrt(jnp.mean(v * v, axis=-1, keepdims=True) + EPS)

    y = (yf_ref[0] + yb_ref[0]) * _silu(z_ref[0])
    half = SSD_WIDTH // 2
    ssd = jnp.concatenate([rms(y[:, :half]), rms(y[:, half:])], axis=1) * snw_ref[...]
    out = _dot(ssd.astype(BF16), wout_ref[0:SSD_WIDTH, :])

    o = of_ref[0] + ob_ref[0]
    gla = jnp.concatenate([rms(o[:, h * GLA_DV:(h + 1) * GLA_DV]) for h in range(GLA_HEADS)], axis=1)
    gla = gla * gnw_ref[...] * _silu(gg_ref[0])
    out = out + _dot(gla.astype(BF16), wout_ref[SSD_WIDTH:SSD_WIDTH + GLA_WIDTH, :])

    slabs = range(S5_WIDTH // LANES)
    y5 = jnp.concatenate([sf_ref[0, k] + sb_ref[0, k] for k in slabs], axis=1)
    u5 = jnp.concatenate([u5_ref[0, k] for k in slabs], axis=1)
    y5 = jax.nn.gelu(y5 + s5d_ref[...] * u5)
    pr = _dot(y5.astype(BF16), gluw_ref[...]) + glub_ref[...]
    s5 = pr[:, :S5_WIDTH] * jax.nn.sigmoid(pr[:, S5_WIDTH:]) * _silu(sg_ref[0])
    out = out + _dot(s5.astype(BF16), wout_ref[SSD_WIDTH + GLA_WIDTH:, :])

    h_new = res + gate * out
    if last:
        h_new = rms(h_new) * fnw_ref[...]
    if col_major:
        rows = o_ref.shape[1]
        for k in range(o_ref.shape[2] // D_MODEL):
            o_ref[0, :, k * D_MODEL:(k + 1) * D_MODEL] = h_new[k * rows:(k + 1) * rows, :]
    else:
        o_ref[0] = h_new


def _outproj(lat, ctx, ctx_block, mod_l, y_ssd, z, o_gla, gg, y_s5, u5, sg, ssd_norm_w, gla_norm_w, s5_d,
             glu_w, glu_b, w_out, final_norm_w, *, col_major, last, n_batch, seq, rows):
    n_lat_tiles = seq // TILE
    n_tiles = n_lat_tiles if last else n_lat_tiles + 1
    tok = lambda width: pl.BlockSpec((1, TILE, width), lambda b, j: (b, j, 0))
    both = lambda width: [tok(width), tok(width)]
    slabs = S5_WIDTH // LANES
    s5_tok = pl.BlockSpec((1, slabs, TILE, LANES), lambda b, j: (b, 0, j, 0))
    full = lambda a: pl.BlockSpec(a.shape, lambda b, j: (0,) * a.ndim)
    small = [ssd_norm_w, gla_norm_w, s5_d, glu_w, glu_b, w_out, final_norm_w]
    tokens = (lat,) if last else (lat, ctx)
    if col_major:
        cols = TILE // rows
        out_spec = pl.BlockSpec((1, rows, cols * D_MODEL), lambda b, j: (b, 0, j))
        out_shape = jax.ShapeDtypeStruct((n_batch, rows, (seq // rows) * D_MODEL), F32)
    else:
        out_spec = tok(D_MODEL)
        out_shape = jax.ShapeDtypeStruct((n_batch, n_tiles * TILE, D_MODEL), F32)
    return pl.pallas_call(
        functools.partial(_outproj_body, col_major=col_major, last=last, n_batch=n_batch,
                          n_lat_tiles=n_lat_tiles),
        grid=(n_batch, n_tiles),
        in_specs=_token_specs(n_lat_tiles, rows, col_major, not last, ctx_block)
        + [full(mod_l)] + both(SSD_WIDTH) + [tok(SSD_WIDTH)] + both(GLA_WIDTH) + [tok(GLA_WIDTH)]
        + [s5_tok, s5_tok, s5_tok, tok(S5_WIDTH)]
        + [full(a) for a in small],
        out_specs=out_spec,
        out_shape=out_shape,
        compiler_params=_params(("arbitrary", "arbitrary")),
        name="outproj_col" if col_major else "outproj_row",
    )(*tokens, mod_l, *y_ssd, z, *o_gla, gg, *y_s5, u5, sg, *small)


def _pack_w_in(w):
    sizes = (1024, SSD_CONV_DIM, 32, 256, 256, 512, 512, 32, 512, 512)
    parts, start = [], 0
    for size in sizes:
        parts.append(w[:, start:start + size])
        start += size
    z, xbc, dt, q, k, v, gg, lr, u5, sg = parts
    pad = lambda a: jnp.pad(a, ((0, 0), (0, LANES - a.shape[1])))
    cols = [z, xbc, pad(dt[:, :16]), pad(dt[:, 16:]), q, k, v, gg, pad(lr[:, :16]), pad(lr[:, 16:]), u5, sg]
    return jnp.concatenate(cols, axis=1).astype(BF16)


def kernel(x, c, ctx, c_ctx, norm_w, mod_w, mod_b, w_in, w_out, ssd_conv_w, ssd_conv_b, ssd_a_log, ssd_dt_bias, ssd_d, ssd_norm_w, gla_w_lr, gla_b_lr, gla_norm_w, s5_lam_re, s5_lam_im, s5_log_step, s5_b_re, s5_b_im, s5_c_re, s5_c_im, s5_d, s5_glu_w, s5_glu_b, final_norm_w):
    n_batch, seq, d_model = x.shape
    depth = norm_w.shape[0]
    rows = seq // GRID_W
    assert d_model == D_MODEL and ctx.shape[1] == CTX_LEN == TILE
    assert seq % TILE == 0 and TILE % rows == 0
    assert depth == 2, "layer 0 row-major with context update, layer 1 column-major and last"
    n_lat_tiles = seq // TILE

    mod_rows = -(-(n_batch + 1) // 8) * 8
    cc = jnp.zeros((mod_rows, d_model), F32).at[:n_batch].set(c).at[n_batch].set(c_ctx)
    mod = _modulation(cc, mod_w, mod_b)

    lat, ctx_arr, ctx_block = x, ctx, 0
    for l in range(depth):
        col_major = l % 2 == 1
        last = l == depth - 1
        if col_major:
            lat_in = lat.reshape(n_batch, lat.shape[1] // GRID_W, GRID_W * d_model)
        else:
            lat_in = lat
        z, xbc, dt, qkv, gg, lr, u5, sg = _inproj(
            lat_in, ctx_arr, ctx_block, mod[l], norm_w[l].reshape(1, -1), _pack_w_in(w_in[l]),
            col_major=col_major, n_batch=n_batch, seq=seq, rows=rows)

        xc = _conv(xbc, ssd_conv_w[l], ssd_conv_b[l], seq=seq)
        pad_h = lambda a: jnp.pad(a, ((0, 0), (0, LANES - a.shape[1])))
        y_ssd = _ssd_scan(xc, dt, pad_h(ssd_dt_bias[l]), pad_h(ssd_a_log[l]),
                          jnp.repeat(ssd_d[l], SSD_HEADDIM).reshape(1, -1), seq=seq)

        w_lr = jnp.pad(gla_w_lr[l], ((0, 0), (0, LANES - GLA_RANK), (0, 0)))
        o_gla = _gla_scan(qkv, lr, w_lr, gla_b_lr[l].reshape(2, 1, -1), seq=seq)

        ops = _s5_prep(s5_lam_re[l], s5_lam_im[l], s5_log_step[l], s5_b_re[l], s5_b_im[l],
                       s5_c_re[l], s5_c_im[l])
        ops = _s5_pair_ops(*ops)
        y_s5 = _s5_scan(u5, ops, seq=seq)

        out = _outproj(
            lat_in, ctx_arr, ctx_block, mod[l], y_ssd, z, o_gla, gg, y_s5, u5, sg,
            ssd_norm_w[l].reshape(1, -1), jnp.tile(gla_norm_w[l], GLA_HEADS).reshape(1, -1),
            s5_d[l].reshape(1, -1), s5_glu_w[l].astype(BF16), s5_glu_b[l].reshape(1, -1),
            w_out[l].astype(BF16), final_norm_w.reshape(1, -1),
            col_major=col_major, last=last, n_batch=n_batch, seq=seq, rows=rows)
        if last:
            return out.reshape(n_batch, seq, d_model)
        lat, ctx_arr, ctx_block = out, out, n_lat_tiles
```

```python
import functools

import jax
import jax.numpy as jnp
from jax import lax
from jax.experimental import pallas as pl
from jax.experimental.pallas import tpu as pltpu

F32, BF16 = jnp.float32, jnp.bfloat16
STORE = BF16
HIGHEST = lax.Precision.HIGHEST
EPS = 1e-6

D_MODEL = 1024
GRID_W = 64
CTX_LEN = 256
TILE = 256
COL_W = 8
LANES = 128
VMEM_LIMIT = 56 * 1024 * 1024

SSD_WIDTH, SSD_HEADS, SSD_HEADDIM, SSD_STATE, SSD_CHUNK = 1024, 16, 64, 128, 128
SSD_CONV_DIM, SSD_CONV_K = 1536, 5
GLA_HEADS, GLA_DK, GLA_DV, GLA_RANK, GLA_TAU, GLA_CHUNK = 4, 64, 128, 16, 16.0, 64
GLA_WIDTH = GLA_HEADS * GLA_DV
S5_WIDTH, S5_GROUP, S5_GROUPS, S5_STATE = 512, 16, 32, 64
S5_T = LANES // S5_GROUP
MIX_WIDTH = SSD_WIDTH + GLA_WIDTH + S5_WIDTH

C_Z, C_XBC, C_SMALL = 0, 1024, 2560
C_QKV, C_GG, C_U5, C_SG, C_END = 2688, 3712, 4224, 4736, 5248
SMALL_DT, SMALL_LR = 0, 32


def _params(semantics):
    return pltpu.CompilerParams(dimension_semantics=semantics, vmem_limit_bytes=VMEM_LIMIT)


def _silu(v):
    return v * jax.nn.sigmoid(v)


def _dot(a, b):
    return jnp.dot(a, b, preferred_element_type=F32)


def _dot_nt(a, b):
    return lax.dot_general(a, b, (((1,), (1,)), ((), ())), preferred_element_type=F32)


def _dot_exact_lhs(m_bf16, v, terms=3):
    out, rest = None, v
    for _ in range(terms):
        piece = rest.astype(BF16)
        rest = rest - piece.astype(F32)
        out = _dot(m_bf16, piece) if out is None else out + _dot(m_bf16, piece)
    return out


def _iota(shape, dim):
    return lax.broadcasted_iota(jnp.int32, shape, dim)


def _interleave(*stages):
    active = list(stages)
    while active:
        for gen in list(active):
            try:
                next(gen)
            except StopIteration:
                active.remove(gen)


def _mod_body(cc_ref, w_ref, b_ref, o_ref):
    s = _silu(cc_ref[...])
    o_ref[0] = jnp.dot(s, w_ref[0], precision=HIGHEST, preferred_element_type=F32) + b_ref[0]


def _modulation(cc, mod_w, mod_b):
    depth, d, three_d = mod_w.shape
    rows = cc.shape[0]
    return pl.pallas_call(
        _mod_body,
        grid=(depth, three_d // d),
        in_specs=[pl.BlockSpec((rows, d), lambda l, n: (0, 0)),
                  pl.BlockSpec((1, d, d), lambda l, n: (l, 0, n)),
                  pl.BlockSpec((1, 1, d), lambda l, n: (l, 0, n))],
        out_specs=pl.BlockSpec((1, rows, d), lambda l, n: (l, 0, n)),
        out_shape=jax.ShapeDtypeStruct((depth, rows, three_d), F32),
        compiler_params=_params(("arbitrary", "arbitrary")),
        name="modulation",
    )(cc, mod_w, mod_b.reshape(depth, 1, three_d))


def _col_base(j, rows):
    cols = TILE // rows
    return (j % (COL_W // cols)) * cols


def _load_tile(lat_ref, ctx_ref, col_major, is_ctx, j):
    if col_major:
        rows = lat_ref.shape[1]
        base = _col_base(j, rows)
        lat = jnp.concatenate([lat_ref[0, :, base + k, :] for k in range(TILE // rows)], axis=0)
    else:
        lat = lat_ref[0]
    if ctx_ref is None:
        return lat
    return jnp.where(is_ctx, ctx_ref[0], lat)


def _token_specs(n_lat_tiles, rows, col_major, with_ctx, ctx_block):
    last = n_lat_tiles - 1
    if col_major:
        per_block = COL_W // (TILE // rows)
        lat = pl.BlockSpec((1, rows, COL_W, D_MODEL), lambda b, j: (b, 0, jnp.minimum(j, last) // per_block, 0))
    else:
        lat = pl.BlockSpec((1, TILE, D_MODEL), lambda b, j: (b, jnp.minimum(j, last), 0))
    if not with_ctx:
        return [lat]
    return [lat, pl.BlockSpec((1, TILE, D_MODEL), lambda b, j: (b, ctx_block, 0))]


def _inproj_body(*refs, col_major, n_batch, n_lat_tiles):
    lat_ref, ctx_ref, mod_ref, nw_ref, w_ref = refs[:5]
    z_ref, xbc_ref, small_ref, qkv_ref, gg_ref, u5_ref, sg_ref = refs[5:]
    b, j = pl.program_id(0), pl.program_id(1)
    is_ctx = j == n_lat_tiles
    x = _load_tile(lat_ref, ctx_ref, col_major, is_ctx, jnp.minimum(j, n_lat_tiles - 1))
    m = mod_ref[pl.ds(jnp.where(is_ctx, n_batch, b), 1), :]
    shift, scale = m[:, :D_MODEL], m[:, D_MODEL:2 * D_MODEL]
    u = x * lax.rsqrt(jnp.mean(x * x, axis=-1, keepdims=True) + EPS) * nw_ref[...]
    u = (u * (1.0 + scale) + shift).astype(BF16)
    z_ref[0] = _dot(u, w_ref[:, C_Z:C_XBC]).astype(z_ref.dtype)
    xbc_ref[0] = _dot(u, w_ref[:, C_XBC:C_SMALL]).astype(xbc_ref.dtype)
    small_ref[0] = _dot(u, w_ref[:, C_SMALL:C_QKV])
    qkv_ref[0] = _dot(u, w_ref[:, C_QKV:C_GG]).astype(qkv_ref.dtype)
    gg_ref[0] = _dot(u, w_ref[:, C_GG:C_U5]).astype(gg_ref.dtype)
    u5 = _dot(u, w_ref[:, C_U5:C_SG])
    for k in range(S5_WIDTH // LANES):
        u5_ref[0, k] = u5[:, k * LANES:(k + 1) * LANES]
    sg_ref[0] = _dot(u, w_ref[:, C_SG:C_END]).astype(sg_ref.dtype)


def _inproj(lat, ctx, ctx_block, mod_l, norm_w, w_packed, *, col_major, n_batch, seq, rows):
    n_lat_tiles = seq // TILE
    lt = seq + CTX_LEN
    tok = lambda width: pl.BlockSpec((1, TILE, width), lambda b, j: (b, j, 0))
    full = lambda a: pl.BlockSpec(a.shape, lambda b, j: (0,) * a.ndim)
    shape = lambda width, dtype=STORE: jax.ShapeDtypeStruct((n_batch, lt, width), dtype)
    return pl.pallas_call(
        functools.partial(_inproj_body, col_major=col_major, n_batch=n_batch, n_lat_tiles=n_lat_tiles),
        grid=(n_batch, n_lat_tiles + 1),
        in_specs=_token_specs(n_lat_tiles, rows, col_major, True, ctx_block)
        + [full(mod_l), full(norm_w), full(w_packed)],
        out_specs=[tok(1024), tok(SSD_CONV_DIM), tok(LANES), tok(1024), tok(512),
                   pl.BlockSpec((1, S5_WIDTH // LANES, TILE, LANES), lambda b, j: (b, 0, j, 0)),
                   tok(512)],
        out_shape=[shape(1024), shape(SSD_CONV_DIM), shape(LANES, F32), shape(1024), shape(512),
                   jax.ShapeDtypeStruct((n_batch, S5_WIDTH // LANES, lt, LANES), F32), shape(512)],
        compiler_params=_params(("arbitrary", "arbitrary")),
        name="inproj_col" if col_major else "inproj_row",
    )(lat, ctx, mod_l, norm_w, w_packed)


CONV_ROWS = 256
CONV_COLS = 256
HALO = 16


def _conv_body(x_ref, w_ref, b_ref, o_ref, *, seq, total):
    n_chunks = total // CONV_ROWS
    lat_chunks = seq // CONV_ROWS
    w = w_ref[...]
    bias = b_ref[...]

    def chunk(i, carry):
        r0 = pl.multiple_of(i * CONV_ROWS, CONV_ROWS)
        main = x_ref[0, pl.ds(r0, CONV_ROWS), :].astype(F32)
        at_start = (i == 0) | (i == lat_chunks)
        at_end = (i == lat_chunks - 1) | (i == n_chunks - 1)
        p0 = pl.multiple_of(jnp.maximum(r0 - HALO, 0), HALO)
        n0 = pl.multiple_of(jnp.minimum(r0 + CONV_ROWS, total - HALO), HALO)
        prev = jnp.where(at_start, 0.0, x_ref[0, pl.ds(p0, HALO), :].astype(F32))
        nxt = jnp.where(at_end, 0.0, x_ref[0, pl.ds(n0, HALO), :].astype(F32))
        ext = jnp.concatenate([prev, main, nxt], axis=0)
        acc = jnp.broadcast_to(bias, main.shape)
        for k in range(SSD_CONV_K):
            off = HALO + k - SSD_CONV_K // 2
            acc = acc + w[k:k + 1, :] * ext[off:off + CONV_ROWS, :]
        o_ref[0, pl.ds(r0, CONV_ROWS), :] = _silu(acc).astype(o_ref.dtype)
        return carry

    lax.fori_loop(0, n_chunks, chunk, 0)


def _conv(xbc, conv_w, conv_b, *, seq):
    n_batch, total, ch = xbc.shape
    return pl.pallas_call(
        functools.partial(_conv_body, seq=seq, total=total),
        grid=(n_batch, ch // CONV_COLS),
        in_specs=[pl.BlockSpec((1, total, CONV_COLS), lambda b, c: (b, 0, c)),
                  pl.BlockSpec((SSD_CONV_K, CONV_COLS), lambda b, c: (0, c)),
                  pl.BlockSpec((1, CONV_COLS), lambda b, c: (0, c))],
        out_specs=pl.BlockSpec((1, total, CONV_COLS), lambda b, c: (b, 0, c)),
        out_shape=jax.ShapeDtypeStruct(xbc.shape, STORE),
        compiler_params=_params(("arbitrary", "arbitrary")),
        name="ssd_conv",
    )(xbc, conv_w, conv_b.reshape(1, ch))


def _ssd_body(xcf_ref, dtf_ref, xcb_ref, dtb_ref, bias_ref, alog_ref, dsk_ref, yf_ref, yb_ref, stf_ref, stb_ref):
    @pl.when(pl.program_id(1) == 0)
    def _():
        stf_ref[...] = jnp.zeros_like(stf_ref)
        stb_ref[...] = jnp.zeros_like(stb_ref)

    _interleave(_ssd_dir(xcf_ref, dtf_ref, bias_ref, alog_ref, dsk_ref, yf_ref, stf_ref, rev=False),
                _ssd_dir(xcb_ref, dtb_ref, bias_ref, alog_ref, dsk_ref, yb_ref, stb_ref, rev=True))


def _ssd_dir(xc_ref, dt_ref, bias_ref, alog_ref, dsk_ref, y_ref, st_ref, *, rev):
    t = SSD_CHUNK
    d = 1 if rev else 0
    off = SMALL_DT + SSD_HEADS * d
    dt = jax.nn.softplus(dt_ref[0] + bias_ref[...])
    da = dt * -jnp.exp(alog_ref[...])
    row, col = _iota((t, t), 0), _iota((t, t), 1)
    tri = row <= col if rev else row >= col
    p = _dot_exact_lhs(jnp.where(tri, 1.0, 0.0).astype(BF16), da)
    total = jnp.sum(da, axis=0, keepdims=True)
    e_tot = jnp.exp(total)
    p_t = p.T
    dt_t = dt.T
    w_t = (dt * jnp.exp(total - p)).T
    lo = _iota((1, LANES), 1) < SSD_HEADDIM
    hi = jnp.logical_not(lo)
    pairs_per_group = SSD_HEADS // 2 // 2
    yield
    for g in range(2):
        bg = xc_ref[0, :, SSD_WIDTH + g * SSD_STATE:SSD_WIDTH + (g + 1) * SSD_STATE]
        cg = xc_ref[0, :, SSD_WIDTH + (2 + g) * SSD_STATE:SSD_WIDTH + (3 + g) * SSD_STATE]
        cg_b = cg.astype(BF16)
        cb = _dot_nt(cg_b, bg.astype(BF16))
        bg_t = bg.astype(F32).T
        for q in range(pairs_per_group):
            hp = g * pairs_per_group + q
            h0, h1 = off + 2 * hp, off + 2 * hp + 1
            pb0 = jnp.broadcast_to(p[:, h0:h0 + 1], (t, LANES))
            pb1 = jnp.broadcast_to(p[:, h1:h1 + 1], (t, LANES))
            xp = xc_ref[0, :, hp * LANES:(hp + 1) * LANES]
            st = st_ref[hp]
            acc = jnp.exp(jnp.where(lo, pb0, pb1)) * _dot(cg_b, st.astype(BF16))
            if not rev:
                acc = acc + dsk_ref[:, hp * LANES:(hp + 1) * LANES] * xp.astype(F32)
            st = jnp.where(lo, e_tot[:, h0:h0 + 1], e_tot[:, h1:h1 + 1]) * st
            for h, pb, sel in ((h0, pb0, lo), (h1, pb1, hi)):
                x_h = jnp.where(sel, xp, jnp.zeros_like(xp)).astype(BF16)
                decay = jnp.where(tri, jnp.exp(pb - p_t[h:h + 1, :]), 0.0)
                acc = acc + _dot((cb * decay * dt_t[h:h + 1, :]).astype(BF16), x_h)
                st = st + _dot((bg_t * w_t[h:h + 1, :]).astype(BF16), x_h)
            y_ref[0, :, hp * LANES:(hp + 1) * LANES] = acc.astype(y_ref.dtype)
            st_ref[hp] = st
            yield


def _ssd_scan(xc, dt, dt_bias, a_log, dsk, *, seq):
    n_batch, total, _ = xc.shape
    nc = total // SSD_CHUNK
    lat_c = seq // SSD_CHUNK
    ctx_c = nc - lat_c
    fwd = lambda i: jnp.where(i < ctx_c, lat_c + i, i - ctx_c)
    bwd = lambda i: nc - 1 - i
    full = lambda a: pl.BlockSpec(a.shape, lambda b, i: (0,) * a.ndim)
    xc_spec = lambda chunk: pl.BlockSpec((1, SSD_CHUNK, SSD_CONV_DIM), lambda b, i: (b, chunk(i), 0))
    dt_spec = lambda chunk: pl.BlockSpec((1, SSD_CHUNK, LANES), lambda b, i: (b, chunk(i), 0))
    y_spec = lambda chunk: pl.BlockSpec((1, SSD_CHUNK, SSD_WIDTH), lambda b, i: (b, chunk(i), 0))
    y_shape = jax.ShapeDtypeStruct((n_batch, total, SSD_WIDTH), STORE)
    return pl.pallas_call(
        _ssd_body,
        grid=(n_batch, nc),
        in_specs=[xc_spec(fwd), dt_spec(fwd), xc_spec(bwd), dt_spec(bwd),
                  full(dt_bias), full(a_log), full(dsk)],
        out_specs=[y_spec(fwd), y_spec(bwd)],
        out_shape=[y_shape, y_shape],
        scratch_shapes=[pltpu.VMEM((SSD_HEADS // 2, SSD_STATE, LANES), F32)] * 2,
        compiler_params=_params(("arbitrary", "arbitrary")),
        name="ssd_scan",
    )(xc, dt, xc, dt, dt_bias, a_log, dsk)


def _tile_index(d, i, n_lat_tiles):
    return jnp.where(i == 0, n_lat_tiles, jnp.where(d == 0, i - 1, n_lat_tiles - i))


def _gla_body(qkvf_ref, lrf_ref, qkvb_ref, lrb_ref, wlr_ref, blr_ref, of_ref, ob_ref, sf_ref, sb_ref):
    @pl.when(pl.program_id(1) == 0)
    def _():
        sf_ref[...] = jnp.zeros_like(sf_ref)
        sb_ref[...] = jnp.zeros_like(sb_ref)

    _interleave(_gla_dir(qkvf_ref, lrf_ref, wlr_ref, blr_ref, of_ref, sf_ref, rev=False),
                _gla_dir(qkvb_ref, lrb_ref, wlr_ref, blr_ref, ob_ref, sb_ref, rev=True))


def _gla_dir(qkv_ref, lr_ref, wlr_ref, blr_ref, o_ref, s_ref, *, rev):
    d = 1 if rev else 0
    t = GLA_CHUNK
    n_chunks = TILE // t
    kw = GLA_HEADS * GLA_DK
    row, col = _iota((TILE, TILE), 0), _iota((TILE, TILE), 1)
    same = row // t == col // t
    tri = jnp.where(same, row - col, 1 if rev else -1)
    tri = tri <= 0 if rev else tri >= 0
    lo = _iota((1, LANES), 1) < GLA_DK
    hi = jnp.logical_not(lo)
    eye = _iota((LANES, LANES), 0) == _iota((LANES, LANES), 1)
    row_lo = _iota((LANES, 1), 0) < GLA_DK

    logit = jnp.dot(lr_ref[0], wlr_ref[d], precision=HIGHEST, preferred_element_type=F32) + blr_ref[d]
    yield
    g = jax.nn.log_sigmoid(logit) / GLA_TAU
    p = _dot_exact_lhs(jnp.where(tri, 1.0, 0.0).astype(BF16), g, terms=2)
    total = jnp.concatenate(
        [jnp.broadcast_to(p[(c * t if rev else (c + 1) * t - 1):(c * t + 1 if rev else (c + 1) * t), :], (t, kw))
         for c in range(n_chunks)], axis=0)
    yield
    q = qkv_ref[0, :, 0:kw].astype(F32) * GLA_DK ** -0.5
    k = qkv_ref[0, :, kw:2 * kw].astype(F32)
    qe = q * jnp.exp(p)
    ke = k * jnp.exp(-p)
    kd = k * jnp.exp(total - p)
    e_tot = jnp.exp(total)
    order = range(n_chunks - 1, -1, -1) if rev else range(n_chunks)
    pairs = range(GLA_HEADS // 2)
    lanes = lambda pr: slice(pr * LANES, (pr + 1) * LANES)
    qms, intra, vs = {}, {}, {}
    for pr in pairs:
        ke_p = ke[:, lanes(pr)].astype(BF16)
        for e, sel in ((0, lo), (1, hi)):
            h = 2 * pr + e
            qms[h] = jnp.where(sel, qe[:, lanes(pr)], 0.0).astype(BF16)
            vs[h] = qkv_ref[0, :, 2 * kw + h * GLA_DV:2 * kw + (h + 1) * GLA_DV].astype(BF16)
            attn = jnp.where(tri, _dot_nt(qms[h], ke_p), 0.0).astype(BF16)
            yield
            intra[h] = _dot(attn, vs[h])
            yield
    upd, dec = {}, {}
    for pr in pairs:
        for c in order:
            rows = slice(c * t, (c + 1) * t)
            kd_t = kd[rows, lanes(pr)].T.astype(BF16)
            upd[pr, c] = jnp.where(row_lo, _dot(kd_t, vs[2 * pr][rows]), _dot(kd_t, vs[2 * pr + 1][rows]))
            dec[pr, c] = jnp.sum(jnp.where(eye, e_tot[c * t:c * t + 1, lanes(pr)], 0.0), axis=1, keepdims=True)
        yield
    for pr in pairs:
        s = s_ref[pr]
        for c in order:
            rows = slice(c * t, (c + 1) * t)
            s_b = s.astype(BF16)
            for h in (2 * pr, 2 * pr + 1):
                o_ref[0, rows, h * GLA_DV:(h + 1) * GLA_DV] = (intra[h][rows] + _dot(qms[h][rows], s_b)).astype(o_ref.dtype)
            s = dec[pr, c] * s + upd[pr, c]
            yield
        s_ref[pr] = s


def _gla_scan(qkv, lr, w_lr, b_lr, *, seq):
    n_batch, total, _ = qkv.shape
    n_lat_tiles = seq // TILE
    tile = lambda d: (lambda i: _tile_index(d, i, n_lat_tiles))
    full = lambda a: pl.BlockSpec(a.shape, lambda b, i: (0,) * a.ndim)
    qkv_spec = lambda d: pl.BlockSpec((1, TILE, 1024), lambda b, i: (b, tile(d)(i), 0))
    lr_spec = lambda d: pl.BlockSpec((1, TILE, LANES), lambda b, i: (b, tile(d)(i), 0))
    o_spec = lambda d: pl.BlockSpec((1, TILE, GLA_WIDTH), lambda b, i: (b, tile(d)(i), 0))
    o_shape = jax.ShapeDtypeStruct((n_batch, total, GLA_WIDTH), STORE)
    return pl.pallas_call(
        _gla_body,
        grid=(n_batch, n_lat_tiles + 1),
        in_specs=[qkv_spec(0), lr_spec(0), qkv_spec(1), lr_spec(1), full(w_lr), full(b_lr)],
        out_specs=[o_spec(0), o_spec(1)],
        out_shape=[o_shape, o_shape],
        scratch_shapes=[pltpu.VMEM((GLA_HEADS // 2, LANES, GLA_DV), F32)] * 2,
        compiler_params=_params(("arbitrary", "arbitrary")),
        name="gla_scan",
    )(qkv, lr, qkv, lr, w_lr, b_lr)


S5_ROWS = TILE // S5_T


def _s5_prep_body(ls_ref, lrr_ref, lir_ref, lrc_ref, lic_ref, p1_ref, p2_ref, q1_ref, q2_ref,
                  g_ref, win_ref, wout_ref, ar_ref, ai_ref):
    d = pl.program_id(0)
    fwd = d == 0
    n = LANES
    step = jnp.exp(ls_ref[0, 0])
    lo = _iota((1, n), 1) < S5_STATE
    top = _iota((n, 1), 0) < S5_STATE

    lre, lim = lrr_ref[0, 0], lir_ref[0, 0]
    lr, ang = lre * step, lim * step
    er = jnp.exp(lr)
    nr, ni = er * jnp.cos(ang) - 1.0, er * jnp.sin(ang)
    den = lre * lre + lim * lim
    cr, ci = (nr * lre + ni * lim) / den, (ni * lre - nr * lim) / den
    p1, p2 = p1_ref[0], p2_ref[0]
    bbr, bbi = cr * p1 - ci * p2, cr * p2 + ci * p1
    s_idx = _iota((n, 1), 0) // S5_GROUP
    pos_s = jnp.where(fwd, s_idx, S5_T - 1 - s_idx)
    xs = (S5_T - 1 - pos_s).astype(F32)
    e = jnp.exp(xs * lr)
    l_re, l_im = e * jnp.cos(xs * ang), e * jnp.sin(xs * ang)
    w_in = jnp.where(lo, bbr * l_re - bbi * l_im, bbr * l_im + bbi * l_re)
    win_ref[0, 0] = w_in.astype(BF16)

    lrc, angc = lrc_ref[0, 0] * step, lic_ref[0, 0] * step
    q1, q2 = q1_ref[0, 0], q2_ref[0, 0]
    t_idx = _iota((1, n), 1) // S5_GROUP
    pos_t = jnp.where(fwd, t_idx, S5_T - 1 - t_idx)

    def out_op(xt):
        ex = jnp.exp(lrc * xt)
        o_re, o_im = ex * jnp.cos(angc * xt), ex * jnp.sin(angc * xt)
        return jnp.where(top, q1 * o_re - q2 * o_im, -(q1 * o_im + q2 * o_re))

    wout_ref[0, 0] = out_op((pos_t + 1).astype(F32)).astype(BF16)
    g = jnp.dot(w_in, out_op((pos_t - (S5_T - 1)).astype(F32)), precision=HIGHEST,
                preferred_element_type=F32)
    g_ref[0, 0] = jnp.where(pos_t >= pos_s, g, 0.0).astype(BF16)

    r_idx = _iota((S5_ROWS, 1), 0)
    xe = (S5_T * (jnp.where(fwd, r_idx, S5_ROWS - 1 - r_idx) + 1)).astype(F32)
    ee = jnp.exp(xe * lr)
    a_im = ee * jnp.sin(xe * ang)
    ar_ref[0, 0] = ee * jnp.cos(xe * ang)
    ai_ref[0, 0] = jnp.where(lo, -a_im, a_im)


def _s5_prep(lam_re, lam_im, log_step, b_re, b_im, c_re, c_im):
    n = LANES
    dup_l = lambda a: jnp.concatenate([a, a], axis=-1)
    dup_r = lambda a: jnp.concatenate([a, a], axis=-2)
    lrr, lir = dup_l(lam_re)[:, :, None, :], dup_l(lam_im)[:, :, None, :]
    lrc, lic = lrr.reshape(2, S5_GROUPS, n, 1), lir.reshape(2, S5_GROUPS, n, 1)
    ls = log_step.reshape(2, S5_GROUPS, 1, 1)
    tile_rows = lambda a: dup_l(jnp.tile(jnp.swapaxes(a, -1, -2), (1, S5_T, 1)))
    p1, p2 = tile_rows(b_re), tile_rows(b_im)
    tile_cols = lambda a: dup_r(jnp.tile(jnp.swapaxes(a, -1, -2), (1, 1, 1, S5_T)))
    q1, q2 = tile_cols(c_re), tile_cols(c_im)
    dg = lambda *blk: pl.BlockSpec((1, 1) + blk, lambda d, g: (d, g, 0, 0))
    gg = lambda *blk: pl.BlockSpec((1,) + blk, lambda d, g: (g, 0, 0))
    out = lambda rows, dtype: jax.ShapeDtypeStruct((2, S5_GROUPS, rows, n), dtype)
    return pl.pallas_call(
        _s5_prep_body,
        grid=(2, S5_GROUPS),
        in_specs=[dg(1, 1), dg(1, n), dg(1, n), dg(n, 1), dg(n, 1), gg(n, n), gg(n, n), dg(n, n), dg(n, n)],
        out_specs=[dg(n, n), dg(n, n), dg(n, n), dg(S5_ROWS, n), dg(S5_ROWS, n)],
        out_shape=[out(n, BF16), out(n, BF16), out(n, BF16), out(S5_ROWS, F32), out(S5_ROWS, F32)],
        compiler_params=_params(("arbitrary", "arbitrary")),
        name="s5_prep",
    )(ls, lrr, lir, lrc, lic, p1, p2, q1, q2)


def _s5_pair_ops(g, win, wout, ar, ai):
    hs = S5_STATE
    cat = jnp.concatenate
    pair = lambda a: (a[:, 0::2], a[:, 1::2])
    g0, g1 = pair(g)
    i0, i1 = pair(win)
    o0, o1 = pair(wout)
    z = jnp.zeros_like(g0)
    zh, zr = z[..., :hs], z[..., :hs, :]
    wa = cat([cat([g0, z, i0[..., :hs], zh, i0[..., hs:], zh], -1),
              cat([z, g1, zh, i1[..., :hs], zh, i1[..., hs:]], -1)], -2)
    wo = cat([cat([o0[..., :hs, :], zr], -1), cat([zr, o1[..., :hs, :]], -1),
              cat([o0[..., hs:, :], zr], -1), cat([zr, o1[..., hs:, :]], -1)], -2)
    a0, a1 = pair(ar)
    b0, b1 = pair(ai)
    return wa, wo, cat([a0[..., :hs], a1[..., :hs]], -1), cat([b0[..., hs:], b1[..., hs:]], -1)


def _s5_body(uf_ref, ub_ref, wa_ref, wo_ref, are_ref, aim_ref, yf_ref, yb_ref, *h_refs):
    @pl.when(pl.program_id(1) == 0)
    def _():
        for h_ref in h_refs:
            h_ref[...] = jnp.zeros_like(h_ref)

    ops = (wa_ref, wo_ref, are_ref, aim_ref)
    slab_f = _s5_dir(uf_ref, *ops, yf_ref, *h_refs[:2], rev=False)
    slab_b = _s5_dir(ub_ref, *ops, yb_ref, *h_refs[2:], rev=True)

    def slab(k, carry):
        _interleave(slab_f(k), slab_b(k))
        return carry

    lax.fori_loop(0, S5_WIDTH // LANES, slab, 0)


def _s5_dir(u_ref, wa_ref, wo_ref, are_ref, aim_ref, y_ref, hre_ref, him_ref, *, rev):
    d = 1 if rev else 0
    rws = S5_ROWS
    lane_t = _iota((1, LANES), 1) // S5_GROUP
    row = _iota((rws, 1), 0)
    pos = rws - 1 - row if rev else row
    last_row = 0 if rev else rws - 1
    per_slab = LANES // S5_GROUP
    pairs = per_slab // 2

    def earlier(v, k):
        return pltpu.roll(v, rws - k if rev else k, 0)

    def slab(k):
        parts = [u_ref[0, k, pl.ds(t, rws, stride=S5_T), :] for t in range(S5_T)]
        h_all_re, h_all_im = hre_ref[k], him_ref[k]
        xs = []
        for gi in range(per_slab):
            x = None
            for t in range(S5_T):
                sh = ((t - gi) * S5_GROUP) % LANES
                piece = parts[t] if sh == 0 else pltpu.roll(parts[t], sh, 1)
                x = piece if x is None else jnp.where(lane_t == t, piece, x)
            xs.append(x)
        yield
        first = []
        for q in range(pairs):
            x2 = jnp.concatenate([xs[2 * q], xs[2 * q + 1]], axis=1).astype(BF16)
            first.append(_dot(x2, wa_ref[d, k * pairs + q]))
            yield
        ys, new_re, new_im = [], [], []
        for q in range(pairs):
            pr = k * pairs + q
            res = first[q]
            y_intra, h_re, h_im = res[:, :2 * LANES], res[:, 2 * LANES:3 * LANES], res[:, 3 * LANES:]
            span = 1
            while span < rws:
                idx = rws - span if rev else span - 1
                a_re, a_im = are_ref[d, pr, idx:idx + 1, :], aim_ref[d, pr, idx:idx + 1, :]
                s_re = jnp.where(pos >= span, earlier(h_re, span), 0.0)
                s_im = jnp.where(pos >= span, earlier(h_im, span), 0.0)
                h_re, h_im = h_re + (a_re * s_re - a_im * s_im), h_im + (a_re * s_im + a_im * s_re)
                span *= 2
            in_re = jnp.broadcast_to(h_all_re[8 * q:8 * q + 1, :], (rws, LANES))
            in_im = jnp.broadcast_to(h_all_im[8 * q:8 * q + 1, :], (rws, LANES))
            t_re, t_im = are_ref[d, pr], aim_ref[d, pr]
            h_re, h_im = h_re + (t_re * in_re - t_im * in_im), h_im + (t_re * in_im + t_im * in_re)
            p_re = jnp.where(pos == 0, in_re, earlier(h_re, 1))
            p_im = jnp.where(pos == 0, in_im, earlier(h_im, 1))
            new_re.append(jnp.broadcast_to(h_re[last_row:last_row + 1, :], (8, LANES)))
            new_im.append(jnp.broadcast_to(h_im[last_row:last_row + 1, :], (8, LANES)))
            y2 = y_intra + _dot(jnp.concatenate([p_re, p_im], axis=1).astype(BF16), wo_ref[d, pr])
            ys += [y2[:, :LANES], y2[:, LANES:]]
            yield
        hre_ref[k] = jnp.concatenate(new_re, axis=0)
        him_ref[k] = jnp.concatenate(new_im, axis=0)
        for t in range(S5_T):
            out = None
            for gi in range(per_slab):
                sh = ((gi - t) * S5_GROUP) % LANES
                piece = ys[gi] if sh == 0 else pltpu.roll(ys[gi], sh, 1)
                out = piece if out is None else jnp.where(lane_t == gi, piece, out)
            y_ref[0, k, pl.ds(t, rws, stride=S5_T), :] = out

    return slab


def _s5_scan(u5, ops, *, seq):
    n_batch, slabs, total, _ = u5.shape
    n_lat_tiles = seq // TILE
    tok = lambda d: pl.BlockSpec((1, slabs, TILE, LANES), lambda b, i: (b, 0, _tile_index(d, i, n_lat_tiles), 0))
    full = lambda a: pl.BlockSpec(a.shape, lambda b, i: (0,) * a.ndim)
    pairs_per_slab = LANES // S5_GROUP // 2
    y_shape = jax.ShapeDtypeStruct((n_batch, slabs, total, LANES), F32)
    return pl.pallas_call(
        _s5_body,
        grid=(n_batch, n_lat_tiles + 1),
        in_specs=[tok(0), tok(1)] + [full(a) for a in ops],
        out_specs=[tok(0), tok(1)],
        out_shape=[y_shape, y_shape],
        scratch_shapes=[pltpu.VMEM((slabs, 8 * pairs_per_slab, LANES), F32)] * 4,
        compiler_params=_params(("arbitrary", "arbitrary")),
        name="s5_scan",
    )(u5, u5, *ops)


def _outproj_body(*refs, col_major, last, n_batch, n_lat_tiles):
    n_tok = 1 if last else 2
    lat_ref = refs[0]
    ctx_ref = None if last else refs[1]
    (mod_ref, yf_ref, yb_ref, z_ref, of_ref, ob_ref, gg_ref, sf_ref, sb_ref, u5_ref, sg_ref,
     snw_ref, gnw_ref, s5d_ref, gluw_ref, glub_ref, wout_ref, fnw_ref, o_ref) = refs[n_tok:]
    b, j = pl.program_id(0), pl.program_id(1)
    is_ctx = j == n_lat_tiles
    res = _load_tile(lat_ref, ctx_ref, col_major, is_ctx, jnp.minimum(j, n_lat_tiles - 1))
    gate = mod_ref[pl.ds(jnp.where(is_ctx, n_batch, b), 1), 2 * D_MODEL:3 * D_MODEL]

    def rms(v):
        return v * lax.rsqrt(jnp.mean(v * v, axis=-1, keepdims=True) + EPS)

    y = (yf_ref[0].astype(F32) + yb_ref[0].astype(F32)) * _silu(z_ref[0].astype(F32))
    half = SSD_WIDTH // 2
    ssd = jnp.concatenate([rms(y[:, :half]), rms(y[:, half:])], axis=1) * snw_ref[...]
    out = _dot(ssd.astype(BF16), wout_ref[0:SSD_WIDTH, :])

    o = of_ref[0].astype(F32) + ob_ref[0].astype(F32)
    gla = jnp.concatenate([rms(o[:, h * GLA_DV:(h + 1) * GLA_DV]) for h in range(GLA_HEADS)], axis=1)
    gla = gla * gnw_ref[...] * _silu(gg_ref[0].astype(F32))
    out = out + _dot(gla.astype(BF16), wout_ref[SSD_WIDTH:SSD_WIDTH + GLA_WIDTH, :])

    slabs = range(S5_WIDTH // LANES)
    y5 = jnp.concatenate([sf_ref[0, k] + sb_ref[0, k] for k in slabs], axis=1)
    u5 = jnp.concatenate([u5_ref[0, k] for k in slabs], axis=1)
    y5 = jax.nn.gelu(y5 + s5d_ref[...] * u5)
    pr = _dot(y5.astype(BF16), gluw_ref[...]) + glub_ref[...]
    s5 = pr[:, :S5_WIDTH] * jax.nn.sigmoid(pr[:, S5_WIDTH:]) * _silu(sg_ref[0].astype(F32))
    out = out + _dot(s5.astype(BF16), wout_ref[SSD_WIDTH + GLA_WIDTH:, :])

    h_new = res + gate * out
    if last:
        h_new = rms(h_new) * fnw_ref[...]
    if col_major:
        rows = o_ref.shape[1]
        base = _col_base(j, rows)
        for k in range(TILE // rows):
            o_ref[0, :, base + k, :] = h_new[k * rows:(k + 1) * rows, :]
    else:
        o_ref[0] = h_new


def _outproj(lat, ctx, ctx_block, mod_l, y_ssd, z, o_gla, gg, y_s5, u5, sg, ssd_norm_w, gla_norm_w, s5_d,
             glu_w, glu_b, w_out, final_norm_w, *, col_major, last, n_batch, seq, rows):
    n_lat_tiles = seq // TILE
    n_tiles = n_lat_tiles if last else n_lat_tiles + 1
    tok = lambda width: pl.BlockSpec((1, TILE, width), lambda b, j: (b, j, 0))
    both = lambda width: [tok(width), tok(width)]
    slabs = S5_WIDTH // LANES
    s5_tok = pl.BlockSpec((1, slabs, TILE, LANES), lambda b, j: (b, 0, j, 0))
    full = lambda a: pl.BlockSpec(a.shape, lambda b, j: (0,) * a.ndim)
    small = [ssd_norm_w, gla_norm_w, s5_d, glu_w, glu_b, w_out, final_norm_w]
    tokens = (lat,) if last else (lat, ctx)
    if col_major:
        per_block = COL_W // (TILE // rows)
        out_spec = pl.BlockSpec((1, rows, COL_W, D_MODEL), lambda b, j: (b, 0, j // per_block, 0))
        out_shape = jax.ShapeDtypeStruct((n_batch, rows, seq // rows, D_MODEL), F32)
    else:
        out_spec = tok(D_MODEL)
        out_shape = jax.ShapeDtypeStruct((n_batch, n_tiles * TILE, D_MODEL), F32)
    return pl.pallas_call(
        functools.partial(_outproj_body, col_major=col_major, last=last, n_batch=n_batch,
                          n_lat_tiles=n_lat_tiles),
        grid=(n_batch, n_tiles),
        in_specs=_token_specs(n_lat_tiles, rows, col_major, not last, ctx_block)
        + [full(mod_l)] + both(SSD_WIDTH) + [tok(SSD_WIDTH)] + both(GLA_WIDTH) + [tok(GLA_WIDTH)]
        + [s5_tok, s5_tok, s5_tok, tok(S5_WIDTH)]
        + [full(a) for a in small],
        out_specs=out_spec,
        out_shape=out_shape,
        compiler_params=_params(("arbitrary", "arbitrary")),
        name="outproj_col" if col_major else "outproj_row",
    )(*tokens, mod_l, *y_ssd, z, *o_gla, gg, *y_s5, u5, sg, *small)


def _pack_w_in(w):
    sizes = (1024, SSD_CONV_DIM, 32, 256, 256, 512, 512, 32, 512, 512)
    parts, start = [], 0
    for size in sizes:
        parts.append(w[:, start:start + size])
        start += size
    z, xbc, dt, q, k, v, gg, lr, u5, sg = parts
    pad = lambda a: jnp.pad(a, ((0, 0), (0, LANES - a.shape[1])))
    cols = [z, xbc, pad(jnp.concatenate([dt, lr], axis=1)), q, k, v, gg, u5, sg]
    return jnp.concatenate(cols, axis=1).astype(BF16)


def kernel(x, c, ctx, c_ctx, norm_w, mod_w, mod_b, w_in, w_out, ssd_conv_w, ssd_conv_b, ssd_a_log, ssd_dt_bias, ssd_d, ssd_norm_w, gla_w_lr, gla_b_lr, gla_norm_w, s5_lam_re, s5_lam_im, s5_log_step, s5_b_re, s5_b_im, s5_c_re, s5_c_im, s5_d, s5_glu_w, s5_glu_b, final_norm_w):
    n_batch, seq, d_model = x.shape
    depth = norm_w.shape[0]
    rows = seq // GRID_W
    assert d_model == D_MODEL and ctx.shape[1] == CTX_LEN == TILE
    assert seq % TILE == 0 and TILE % rows == 0
    assert depth == 2, "layer 0 row-major with context update, layer 1 column-major and last"
    n_lat_tiles = seq // TILE

    mod_rows = -(-(n_batch + 1) // 8) * 8
    cc = jnp.zeros((mod_rows, d_model), F32).at[:n_batch].set(c).at[n_batch].set(c_ctx)
    mod = _modulation(cc, mod_w, mod_b)

    lat, ctx_arr, ctx_block = x, ctx, 0
    for l in range(depth):
        col_major = l % 2 == 1
        last = l == depth - 1
        if col_major:
            lat_in = lat.reshape(n_batch, lat.shape[1] // GRID_W, GRID_W, d_model)
        else:
            lat_in = lat
        z, xbc, small, qkv, gg, u5, sg = _inproj(
            lat_in, ctx_arr, ctx_block, mod[l], norm_w[l].reshape(1, -1), _pack_w_in(w_in[l]),
            col_major=col_major, n_batch=n_batch, seq=seq, rows=rows)

        xc = _conv(xbc, ssd_conv_w[l], ssd_conv_b[l], seq=seq)
        pad_h = lambda a: jnp.pad(a.reshape(1, -1), ((0, 0), (SMALL_DT, LANES - SMALL_DT - a.size)))
        y_ssd = _ssd_scan(xc, small, pad_h(ssd_dt_bias[l]), pad_h(ssd_a_log[l]),
                          jnp.repeat(ssd_d[l], SSD_HEADDIM).reshape(1, -1), seq=seq)

        w_lr = jnp.stack([jnp.pad(gla_w_lr[l, d], ((SMALL_LR + GLA_RANK * d, LANES - SMALL_LR - GLA_RANK * (d + 1)), (0, 0)))
                          for d in (0, 1)])
        o_gla = _gla_scan(qkv, small, w_lr, gla_b_lr[l].reshape(2, 1, -1), seq=seq)

        ops = _s5_prep(s5_lam_re[l], s5_lam_im[l], s5_log_step[l], s5_b_re[l], s5_b_im[l],
                       s5_c_re[l], s5_c_im[l])
        ops = _s5_pair_ops(*ops)
        y_s5 = _s5_scan(u5, ops, seq=seq)

        out = _outproj(
            lat_in, ctx_arr, ctx_block, mod[l], y_ssd, z, o_gla, gg, y_s5, u5, sg,
            ssd_norm_w[l].reshape(1, -1), jnp.tile(gla_norm_w[l], GLA_HEADS).reshape(1, -1),
            s5_d[l].reshape(1, -1), s5_glu_w[l].astype(BF16), s5_glu_b[l].reshape(1, -1),
            w_out[l].astype(BF16), final_norm_w.reshape(1, -1),
            col_major=col_major, last=last, n_batch=n_batch, seq=seq, rows=rows)
        if last:
            return out.reshape(n_batch, seq, d_model)
        lat, ctx_arr, ctx_block = out, out, n_lat_tiles
```

```python
import functools

import jax
import jax.numpy as jnp
from jax import lax
from jax.experimental import pallas as pl
from jax.experimental.pallas import tpu as pltpu

F32, BF16 = jnp.float32, jnp.bfloat16
STORE = BF16
HIGHEST = lax.Precision.HIGHEST
EPS = 1e-6

D_MODEL = 1024
GRID_W = 64
CTX_LEN = 256
TILE = 256
COL_W = 8
LANES = 128
SUB = 8
CONV_COLS = 256
VMEM_LIMIT = 56 * 1024 * 1024

SSD_WIDTH, SSD_HEADS, SSD_HEADDIM, SSD_STATE, SSD_CHUNK = 1024, 16, 64, 128, 128
SSD_CONV_DIM, SSD_CONV_K = 1536, 5
GLA_HEADS, GLA_DK, GLA_DV, GLA_RANK, GLA_TAU, GLA_CHUNK = 4, 64, 128, 16, 16.0, 64
GLA_WIDTH = GLA_HEADS * GLA_DV
S5_WIDTH, S5_GROUP, S5_GROUPS, S5_STATE = 512, 16, 32, 64
S5_T = LANES // S5_GROUP
MIX_WIDTH = SSD_WIDTH + GLA_WIDTH + S5_WIDTH

C_Z, C_XBC, C_SMALL = 0, 1024, 2560
C_QKV, C_GG, C_U5, C_SG, C_END = 2688, 3712, 4224, 4736, 5248
SMALL_DT, SMALL_LR = 0, 32


def _params(semantics):
    return pltpu.CompilerParams(dimension_semantics=semantics, vmem_limit_bytes=VMEM_LIMIT)


def _silu(v):
    return v * jax.nn.sigmoid(v)


def _dot(a, b):
    return jnp.dot(a, b, preferred_element_type=F32)


def _dot_nt(a, b):
    return lax.dot_general(a, b, (((1,), (1,)), ((), ())), preferred_element_type=F32)


def _dot_exact_lhs(m_bf16, v, terms=3):
    out, rest = None, v
    for _ in range(terms):
        piece = rest.astype(BF16)
        rest = rest - piece.astype(F32)
        out = _dot(m_bf16, piece) if out is None else out + _dot(m_bf16, piece)
    return out


def _iota(shape, dim):
    return lax.broadcasted_iota(jnp.int32, shape, dim)


def _interleave(*stages):
    active = list(stages)
    while active:
        for gen in list(active):
            try:
                next(gen)
            except StopIteration:
                active.remove(gen)


def _mod_body(cc_ref, w_ref, b_ref, o_ref):
    s = _silu(cc_ref[...])
    o_ref[0] = jnp.dot(s, w_ref[0], precision=HIGHEST, preferred_element_type=F32) + b_ref[0]


def _modulation(cc, mod_w, mod_b):
    depth, d, three_d = mod_w.shape
    rows = cc.shape[0]
    return pl.pallas_call(
        _mod_body,
        grid=(depth, three_d // d),
        in_specs=[pl.BlockSpec((rows, d), lambda l, n: (0, 0)),
                  pl.BlockSpec((1, d, d), lambda l, n: (l, 0, n)),
                  pl.BlockSpec((1, 1, d), lambda l, n: (l, 0, n))],
        out_specs=pl.BlockSpec((1, rows, d), lambda l, n: (l, 0, n)),
        out_shape=jax.ShapeDtypeStruct((depth, rows, three_d), F32),
        compiler_params=_params(("arbitrary", "arbitrary")),
        name="modulation",
    )(cc, mod_w, mod_b.reshape(depth, 1, three_d))


def _col_base(j, rows):
    cols = TILE // rows
    return (j % (COL_W // cols)) * cols


def _load_tile(lat_ref, ctx_ref, col_major, is_ctx, j):
    if col_major:
        rows = lat_ref.shape[1]
        base = _col_base(j, rows)
        lat = jnp.concatenate([lat_ref[0, :, base + k, :] for k in range(TILE // rows)], axis=0)
    else:
        lat = lat_ref[0]
    if ctx_ref is None:
        return lat
    return jnp.where(is_ctx, ctx_ref[0], lat)


def _token_specs(n_lat_tiles, rows, col_major, with_ctx, ctx_block):
    last = n_lat_tiles - 1
    if col_major:
        per_block = COL_W // (TILE // rows)
        lat = pl.BlockSpec((1, rows, COL_W, D_MODEL), lambda b, j: (b, 0, jnp.minimum(j, last) // per_block, 0))
    else:
        lat = pl.BlockSpec((1, TILE, D_MODEL), lambda b, j: (b, jnp.minimum(j, last), 0))
    if not with_ctx:
        return [lat]
    return [lat, pl.BlockSpec((1, TILE, D_MODEL), lambda b, j: (b, ctx_block, 0))]


def _halo_specs(n_lat_tiles, rows, col_major, seq):
    last = n_lat_tiles - 1
    if col_major:
        cols = TILE // rows
        tile = lambda j: jnp.minimum(j, last)
        prev = pl.BlockSpec((1, SUB, COL_W, D_MODEL),
                            lambda b, j: (b, rows // SUB - 1, jnp.maximum(tile(j) * cols - 1, 0) // COL_W, 0))
        nxt = pl.BlockSpec((1, SUB, COL_W, D_MODEL),
                           lambda b, j: (b, 0, jnp.minimum((tile(j) + 1) * cols, GRID_W - 1) // COL_W, 0))
    else:
        per_tile, n_blk = TILE // SUB, seq // SUB
        tile = lambda j: jnp.minimum(j, last)
        prev = pl.BlockSpec((1, SUB, D_MODEL), lambda b, j: (b, jnp.maximum(tile(j) * per_tile - 1, 0), 0))
        nxt = pl.BlockSpec((1, SUB, D_MODEL), lambda b, j: (b, jnp.minimum((tile(j) + 1) * per_tile, n_blk - 1), 0))
    return [prev, nxt]


def _load_halo(prev_ref, next_ref, col_major, j, rows):
    if col_major:
        cols = TILE // rows
        prev = prev_ref[0, :, jnp.maximum(j * cols - 1, 0) % COL_W, :]
        nxt = next_ref[0, :, jnp.minimum((j + 1) * cols, GRID_W - 1) % COL_W, :]
    else:
        prev, nxt = prev_ref[0], next_ref[0]
    return jnp.concatenate([prev, nxt], axis=0)


def _inproj_body(*refs, col_major, n_batch, n_lat_tiles):
    lat_ref, ctx_ref, prev_ref, next_ref, mod_ref, nw_ref, w_ref, cw_ref, cb_ref = refs[:9]
    z_ref, xc_ref, small_ref, qkv_ref, gg_ref, u5_ref, sg_ref = refs[9:]
    b, j = pl.program_id(0), pl.program_id(1)
    is_ctx = j == n_lat_tiles
    jj = jnp.minimum(j, n_lat_tiles - 1)
    x = _load_tile(lat_ref, ctx_ref, col_major, is_ctx, jj)
    m = mod_ref[pl.ds(jnp.where(is_ctx, n_batch, b), 1), :]
    shift, scale = m[:, :D_MODEL], m[:, D_MODEL:2 * D_MODEL]

    def modulated(v):
        v = v * lax.rsqrt(jnp.mean(v * v, axis=-1, keepdims=True) + EPS) * nw_ref[...]
        return (v * (1.0 + scale) + shift).astype(BF16)

    u = modulated(x)

    u_halo = modulated(_load_halo(prev_ref, next_ref, col_major, jj, lat_ref.shape[1]))
    has_prev = jnp.logical_and(jnp.logical_not(is_ctx), j > 0)
    has_next = jnp.logical_and(jnp.logical_not(is_ctx), j < n_lat_tiles - 1)

    z_ref[0] = _dot(u, w_ref[:, C_Z:C_XBC]).astype(z_ref.dtype)
    for c0 in range(0, SSD_CONV_DIM, CONV_COLS):
        cols = slice(C_XBC + c0, C_XBC + c0 + CONV_COLS)
        halo = _dot(u_halo, w_ref[:, cols])
        ext = jnp.concatenate([jnp.where(has_prev, halo[:SUB], 0.0), _dot(u, w_ref[:, cols]),
                               jnp.where(has_next, halo[SUB:], 0.0)], axis=0)
        acc = jnp.broadcast_to(cb_ref[:, c0:c0 + CONV_COLS], (TILE, CONV_COLS))
        for k in range(SSD_CONV_K):
            off = SUB + k - SSD_CONV_K // 2
            acc = acc + cw_ref[k:k + 1, c0:c0 + CONV_COLS] * ext[off:off + TILE, :]
        xc_ref[0, :, c0:c0 + CONV_COLS] = _silu(acc).astype(xc_ref.dtype)
    small_ref[0] = _dot(u, w_ref[:, C_SMALL:C_QKV])
    qkv_ref[0] = _dot(u, w_ref[:, C_QKV:C_GG]).astype(qkv_ref.dtype)
    gg_ref[0] = _dot(u, w_ref[:, C_GG:C_U5]).astype(gg_ref.dtype)
    u5 = _dot(u, w_ref[:, C_U5:C_SG])
    for k in range(S5_WIDTH // LANES):
        u5_ref[0, k] = u5[:, k * LANES:(k + 1) * LANES]
    sg_ref[0] = _dot(u, w_ref[:, C_SG:C_END]).astype(sg_ref.dtype)


def _inproj(lat, ctx, ctx_block, mod_l, norm_w, w_packed, conv_w, conv_b, *, col_major, n_batch, seq, rows):
    n_lat_tiles = seq // TILE
    lt = seq + CTX_LEN
    tok = lambda width: pl.BlockSpec((1, TILE, width), lambda b, j: (b, j, 0))
    full = lambda a: pl.BlockSpec(a.shape, lambda b, j: (0,) * a.ndim)
    shape = lambda width, dtype=STORE: jax.ShapeDtypeStruct((n_batch, lt, width), dtype)
    return pl.pallas_call(
        functools.partial(_inproj_body, col_major=col_major, n_batch=n_batch, n_lat_tiles=n_lat_tiles),
        grid=(n_batch, n_lat_tiles + 1),
        in_specs=_token_specs(n_lat_tiles, rows, col_major, True, ctx_block)
        + _halo_specs(n_lat_tiles, rows, col_major, seq)
        + [full(mod_l), full(norm_w), full(w_packed), full(conv_w), full(conv_b)],
        out_specs=[tok(1024), tok(SSD_CONV_DIM), tok(LANES), tok(1024), tok(512),
                   pl.BlockSpec((1, S5_WIDTH // LANES, TILE, LANES), lambda b, j: (b, 0, j, 0)),
                   tok(512)],
        out_shape=[shape(1024), shape(SSD_CONV_DIM), shape(LANES, F32), shape(1024), shape(512),
                   jax.ShapeDtypeStruct((n_batch, S5_WIDTH // LANES, lt, LANES), F32), shape(512)],
        compiler_params=_params(("arbitrary", "arbitrary")),
        name="inproj_col" if col_major else "inproj_row",
    )(lat, ctx, lat, lat, mod_l, norm_w, w_packed, conv_w, conv_b)


def _ssd_body(xcf_ref, dtf_ref, xcb_ref, dtb_ref, bias_ref, alog_ref, dsk_ref, yf_ref, yb_ref, stf_ref, stb_ref):
    @pl.when(pl.program_id(1) == 0)
    def _():
        stf_ref[...] = jnp.zeros_like(stf_ref)
        stb_ref[...] = jnp.zeros_like(stb_ref)

    _interleave(_ssd_dir(xcf_ref, dtf_ref, bias_ref, alog_ref, dsk_ref, yf_ref, stf_ref, rev=False),
                _ssd_dir(xcb_ref, dtb_ref, bias_ref, alog_ref, dsk_ref, yb_ref, stb_ref, rev=True))


def _ssd_dir(xc_ref, dt_ref, bias_ref, alog_ref, dsk_ref, y_ref, st_ref, *, rev):
    t = SSD_CHUNK
    d = 1 if rev else 0
    off = SMALL_DT + SSD_HEADS * d
    dt = jax.nn.softplus(dt_ref[0] + bias_ref[...])
    da = dt * -jnp.exp(alog_ref[...])
    row, col = _iota((t, t), 0), _iota((t, t), 1)
    tri = row <= col if rev else row >= col
    p = _dot_exact_lhs(jnp.where(tri, 1.0, 0.0).astype(BF16), da)
    total = jnp.sum(da, axis=0, keepdims=True)
    e_tot = jnp.exp(total)
    p_t = p.T
    dt_t = dt.T
    w_t = (dt * jnp.exp(total - p)).T
    lo = _iota((1, LANES), 1) < SSD_HEADDIM
    hi = jnp.logical_not(lo)
    pairs_per_group = SSD_HEADS // 2 // 2
    yield
    for g in range(2):
        bg = xc_ref[0, :, SSD_WIDTH + g * SSD_STATE:SSD_WIDTH + (g + 1) * SSD_STATE]
        cg = xc_ref[0, :, SSD_WIDTH + (2 + g) * SSD_STATE:SSD_WIDTH + (3 + g) * SSD_STATE]
        cg_b = cg.astype(BF16)
        cb = _dot_nt(cg_b, bg.astype(BF16))
        bg_t = bg.astype(F32).T
        for q in range(pairs_per_group):
            hp = g * pairs_per_group + q
            h0, h1 = off + 2 * hp, off + 2 * hp + 1
            pb0 = jnp.broadcast_to(p[:, h0:h0 + 1], (t, LANES))
            pb1 = jnp.broadcast_to(p[:, h1:h1 + 1], (t, LANES))
            xp = xc_ref[0, :, hp * LANES:(hp + 1) * LANES]
            st = st_ref[hp]
            acc = jnp.exp(jnp.where(lo, pb0, pb1)) * _dot(cg_b, st.astype(BF16))
            if not rev:
                acc = acc + dsk_ref[:, hp * LANES:(hp + 1) * LANES] * xp.astype(F32)
            st = jnp.where(lo, e_tot[:, h0:h0 + 1], e_tot[:, h1:h1 + 1]) * st
            for h, pb, sel in ((h0, pb0, lo), (h1, pb1, hi)):
                x_h = jnp.where(sel, xp, jnp.zeros_like(xp)).astype(BF16)
                decay = jnp.where(tri, jnp.exp(pb - p_t[h:h + 1, :]), 0.0)
                acc = acc + _dot((cb * decay * dt_t[h:h + 1, :]).astype(BF16), x_h)
                st = st + _dot((bg_t * w_t[h:h + 1, :]).astype(BF16), x_h)
            y_ref[0, :, hp * LANES:(hp + 1) * LANES] = acc.astype(y_ref.dtype)
            st_ref[hp] = st
            yield


def _ssd_scan(xc, dt, dt_bias, a_log, dsk, *, seq):
    n_batch, total, _ = xc.shape
    nc = total // SSD_CHUNK
    lat_c = seq // SSD_CHUNK
    ctx_c = nc - lat_c
    fwd = lambda i: jnp.where(i < ctx_c, lat_c + i, i - ctx_c)
    bwd = lambda i: nc - 1 - i
    full = lambda a: pl.BlockSpec(a.shape, lambda b, i: (0,) * a.ndim)
    xc_spec = lambda chunk: pl.BlockSpec((1, SSD_CHUNK, SSD_CONV_DIM), lambda b, i: (b, chunk(i), 0))
    dt_spec = lambda chunk: pl.BlockSpec((1, SSD_CHUNK, LANES), lambda b, i: (b, chunk(i), 0))
    y_spec = lambda chunk: pl.BlockSpec((1, SSD_CHUNK, SSD_WIDTH), lambda b, i: (b, chunk(i), 0))
    y_shape = jax.ShapeDtypeStruct((n_batch, total, SSD_WIDTH), STORE)
    return pl.pallas_call(
        _ssd_body,
        grid=(n_batch, nc),
        in_specs=[xc_spec(fwd), dt_spec(fwd), xc_spec(bwd), dt_spec(bwd),
                  full(dt_bias), full(a_log), full(dsk)],
        out_specs=[y_spec(fwd), y_spec(bwd)],
        out_shape=[y_shape, y_shape],
        scratch_shapes=[pltpu.VMEM((SSD_HEADS // 2, SSD_STATE, LANES), F32)] * 2,
        compiler_params=_params(("arbitrary", "arbitrary")),
        name="ssd_scan",
    )(xc, dt, xc, dt, dt_bias, a_log, dsk)


def _tile_index(d, i, n_lat_tiles):
    return jnp.where(i == 0, n_lat_tiles, jnp.where(d == 0, i - 1, n_lat_tiles - i))


def _gla_body(qkvf_ref, lrf_ref, qkvb_ref, lrb_ref, wlr_ref, blr_ref, of_ref, ob_ref, sf_ref, sb_ref):
    @pl.when(pl.program_id(1) == 0)
    def _():
        sf_ref[...] = jnp.zeros_like(sf_ref)
        sb_ref[...] = jnp.zeros_like(sb_ref)

    _interleave(_gla_dir(qkvf_ref, lrf_ref, wlr_ref, blr_ref, of_ref, sf_ref, rev=False),
                _gla_dir(qkvb_ref, lrb_ref, wlr_ref, blr_ref, ob_ref, sb_ref, rev=True))


def _gla_dir(qkv_ref, lr_ref, wlr_ref, blr_ref, o_ref, s_ref, *, rev):
    d = 1 if rev else 0
    t = GLA_CHUNK
    n_chunks = TILE // t
    kw = GLA_HEADS * GLA_DK
    row, col = _iota((TILE, TILE), 0), _iota((TILE, TILE), 1)
    same = row // t == col // t
    tri = jnp.where(same, row - col, 1 if rev else -1)
    tri = tri <= 0 if rev else tri >= 0
    lo = _iota((1, LANES), 1) < GLA_DK
    hi = jnp.logical_not(lo)
    eye = _iota((LANES, LANES), 0) == _iota((LANES, LANES), 1)
    row_lo = _iota((LANES, 1), 0) < GLA_DK

    logit = jnp.dot(lr_ref[0], wlr_ref[d], precision=HIGHEST, preferred_element_type=F32) + blr_ref[d]
    yield
    g = jax.nn.log_sigmoid(logit) / GLA_TAU
    p = _dot_exact_lhs(jnp.where(tri, 1.0, 0.0).astype(BF16), g, terms=2)
    total = jnp.concatenate(
        [jnp.broadcast_to(p[(c * t if rev else (c + 1) * t - 1):(c * t + 1 if rev else (c + 1) * t), :], (t, kw))
         for c in range(n_chunks)], axis=0)
    yield
    q = qkv_ref[0, :, 0:kw].astype(F32) * GLA_DK ** -0.5
    k = qkv_ref[0, :, kw:2 * kw].astype(F32)
    qe = q * jnp.exp(p)
    ke = k * jnp.exp(-p)
    kd = k * jnp.exp(total - p)
    e_tot = jnp.exp(total)
    order = range(n_chunks - 1, -1, -1) if rev else range(n_chunks)
    pairs = range(GLA_HEADS // 2)
    lanes = lambda pr: slice(pr * LANES, (pr + 1) * LANES)
    qms, intra, vs = {}, {}, {}
    for pr in pairs:
        ke_p = ke[:, lanes(pr)].astype(BF16)
        for e, sel in ((0, lo), (1, hi)):
            h = 2 * pr + e
            qms[h] = jnp.where(sel, qe[:, lanes(pr)], 0.0).astype(BF16)
            vs[h] = qkv_ref[0, :, 2 * kw + h * GLA_DV:2 * kw + (h + 1) * GLA_DV].astype(BF16)
            attn = jnp.where(tri, _dot_nt(qms[h], ke_p), 0.0).astype(BF16)
            yield
            intra[h] = _dot(attn, vs[h])
            yield
    upd, dec = {}, {}
    for pr in pairs:
        for c in order:
            rows = slice(c * t, (c + 1) * t)
            kd_t = kd[rows, lanes(pr)].T.astype(BF16)
            upd[pr, c] = jnp.where(row_lo, _dot(kd_t, vs[2 * pr][rows]), _dot(kd_t, vs[2 * pr + 1][rows]))
            dec[pr, c] = jnp.sum(jnp.where(eye, e_tot[c * t:c * t + 1, lanes(pr)], 0.0), axis=1, keepdims=True)
        yield
    for pr in pairs:
        s = s_ref[pr]
        for c in order:
            rows = slice(c * t, (c + 1) * t)
            s_b = s.astype(BF16)
            for h in (2 * pr, 2 * pr + 1):
                o_ref[0, rows, h * GLA_DV:(h + 1) * GLA_DV] = (intra[h][rows] + _dot(qms[h][rows], s_b)).astype(o_ref.dtype)
            s = dec[pr, c] * s + upd[pr, c]
            yield
        s_ref[pr] = s


def _gla_scan(qkv, lr, w_lr, b_lr, *, seq):
    n_batch, total, _ = qkv.shape
    n_lat_tiles = seq // TILE
    tile = lambda d: (lambda i: _tile_index(d, i, n_lat_tiles))
    full = lambda a: pl.BlockSpec(a.shape, lambda b, i: (0,) * a.ndim)
    qkv_spec = lambda d: pl.BlockSpec((1, TILE, 1024), lambda b, i: (b, tile(d)(i), 0))
    lr_spec = lambda d: pl.BlockSpec((1, TILE, LANES), lambda b, i: (b, tile(d)(i), 0))
    o_spec = lambda d: pl.BlockSpec((1, TILE, GLA_WIDTH), lambda b, i: (b, tile(d)(i), 0))
    o_shape = jax.ShapeDtypeStruct((n_batch, total, GLA_WIDTH), STORE)
    return pl.pallas_call(
        _gla_body,
        grid=(n_batch, n_lat_tiles + 1),
        in_specs=[qkv_spec(0), lr_spec(0), qkv_spec(1), lr_spec(1), full(w_lr), full(b_lr)],
        out_specs=[o_spec(0), o_spec(1)],
        out_shape=[o_shape, o_shape],
        scratch_shapes=[pltpu.VMEM((GLA_HEADS // 2, LANES, GLA_DV), F32)] * 2,
        compiler_params=_params(("arbitrary", "arbitrary")),
        name="gla_scan",
    )(qkv, lr, qkv, lr, w_lr, b_lr)


S5_ROWS = TILE // S5_T
SLABS_IN_FLIGHT = 4


def _s5_prep_body(ls_ref, lrr_ref, lir_ref, lrc_ref, lic_ref, p1_ref, p2_ref, q1_ref, q2_ref,
                  g_ref, win_ref, wout_ref, ar_ref, ai_ref):
    d = pl.program_id(0)
    fwd = d == 0
    n = LANES
    step = jnp.exp(ls_ref[0, 0])
    lo = _iota((1, n), 1) < S5_STATE
    top = _iota((n, 1), 0) < S5_STATE

    lre, lim = lrr_ref[0, 0], lir_ref[0, 0]
    lr, ang = lre * step, lim * step
    er = jnp.exp(lr)
    nr, ni = er * jnp.cos(ang) - 1.0, er * jnp.sin(ang)
    den = lre * lre + lim * lim
    cr, ci = (nr * lre + ni * lim) / den, (ni * lre - nr * lim) / den
    p1, p2 = p1_ref[0], p2_ref[0]
    bbr, bbi = cr * p1 - ci * p2, cr * p2 + ci * p1
    s_idx = _iota((n, 1), 0) // S5_GROUP
    pos_s = jnp.where(fwd, s_idx, S5_T - 1 - s_idx)
    xs = (S5_T - 1 - pos_s).astype(F32)
    e = jnp.exp(xs * lr)
    l_re, l_im = e * jnp.cos(xs * ang), e * jnp.sin(xs * ang)
    w_in = jnp.where(lo, bbr * l_re - bbi * l_im, bbr * l_im + bbi * l_re)
    win_ref[0, 0] = w_in.astype(BF16)

    lrc, angc = lrc_ref[0, 0] * step, lic_ref[0, 0] * step
    q1, q2 = q1_ref[0, 0], q2_ref[0, 0]
    t_idx = _iota((1, n), 1) // S5_GROUP
    pos_t = jnp.where(fwd, t_idx, S5_T - 1 - t_idx)

    def out_op(xt):
        ex = jnp.exp(lrc * xt)
        o_re, o_im = ex * jnp.cos(angc * xt), ex * jnp.sin(angc * xt)
        return jnp.where(top, q1 * o_re - q2 * o_im, -(q1 * o_im + q2 * o_re))

    wout_ref[0, 0] = out_op((pos_t + 1).astype(F32)).astype(BF16)
    g = jnp.dot(w_in, out_op((pos_t - (S5_T - 1)).astype(F32)), precision=HIGHEST,
                preferred_element_type=F32)
    g_ref[0, 0] = jnp.where(pos_t >= pos_s, g, 0.0).astype(BF16)

    r_idx = _iota((S5_ROWS, 1), 0)
    xe = (S5_T * (jnp.where(fwd, r_idx, S5_ROWS - 1 - r_idx) + 1)).astype(F32)
    ee = jnp.exp(xe * lr)
    a_im = ee * jnp.sin(xe * ang)
    ar_ref[0, 0] = ee * jnp.cos(xe * ang)
    ai_ref[0, 0] = jnp.where(lo, -a_im, a_im)


def _s5_prep(lam_re, lam_im, log_step, b_re, b_im, c_re, c_im):
    n = LANES
    dup_l = lambda a: jnp.concatenate([a, a], axis=-1)
    dup_r = lambda a: jnp.concatenate([a, a], axis=-2)
    lrr, lir = dup_l(lam_re)[:, :, None, :], dup_l(lam_im)[:, :, None, :]
    lrc, lic = lrr.reshape(2, S5_GROUPS, n, 1), lir.reshape(2, S5_GROUPS, n, 1)
    ls = log_step.reshape(2, S5_GROUPS, 1, 1)
    tile_rows = lambda a: dup_l(jnp.tile(jnp.swapaxes(a, -1, -2), (1, S5_T, 1)))
    p1, p2 = tile_rows(b_re), tile_rows(b_im)
    tile_cols = lambda a: dup_r(jnp.tile(jnp.swapaxes(a, -1, -2), (1, 1, 1, S5_T)))
    q1, q2 = tile_cols(c_re), tile_cols(c_im)
    dg = lambda *blk: pl.BlockSpec((1, 1) + blk, lambda d, g: (d, g, 0, 0))
    gg = lambda *blk: pl.BlockSpec((1,) + blk, lambda d, g: (g, 0, 0))
    out = lambda rows, dtype: jax.ShapeDtypeStruct((2, S5_GROUPS, rows, n), dtype)
    return pl.pallas_call(
        _s5_prep_body,
        grid=(2, S5_GROUPS),
        in_specs=[dg(1, 1), dg(1, n), dg(1, n), dg(n, 1), dg(n, 1), gg(n, n), gg(n, n), dg(n, n), dg(n, n)],
        out_specs=[dg(n, n), dg(n, n), dg(n, n), dg(S5_ROWS, n), dg(S5_ROWS, n)],
        out_shape=[out(n, BF16), out(n, BF16), out(n, BF16), out(S5_ROWS, F32), out(S5_ROWS, F32)],
        compiler_params=_params(("arbitrary", "arbitrary")),
        name="s5_prep",
    )(ls, lrr, lir, lrc, lic, p1, p2, q1, q2)


def _s5_pair_ops(g, win, wout, ar, ai):
    hs = S5_STATE
    cat = jnp.concatenate
    pair = lambda a: (a[:, 0::2], a[:, 1::2])
    g0, g1 = pair(g)
    i0, i1 = pair(win)
    o0, o1 = pair(wout)
    z = jnp.zeros_like(g0)
    zh, zr = z[..., :hs], z[..., :hs, :]
    wa = cat([cat([g0, z, i0[..., :hs], zh, i0[..., hs:], zh], -1),
              cat([z, g1, zh, i1[..., :hs], zh, i1[..., hs:]], -1)], -2)
    wo = cat([cat([o0[..., :hs, :], zr], -1), cat([zr, o1[..., :hs, :]], -1),
              cat([o0[..., hs:, :], zr], -1), cat([zr, o1[..., hs:, :]], -1)], -2)
    a0, a1 = pair(ar)
    b0, b1 = pair(ai)
    return wa, wo, cat([a0[..., :hs], a1[..., :hs]], -1), cat([b0[..., hs:], b1[..., hs:]], -1)


def _block_transpose(arrs):
    n = len(arrs)
    block = _iota((1, LANES), 1) // S5_GROUP
    arrs = list(arrs)
    s = n // 2
    while s >= 1:
        upper = (block // s) % 2 == 1
        nxt = list(arrs)
        for i in range(n):
            if (i // s) % 2 == 0:
                a, b = arrs[i], arrs[i + s]
                nxt[i] = jnp.where(upper, pltpu.roll(b, s * S5_GROUP, 1), a)
                nxt[i + s] = jnp.where(upper, b, pltpu.roll(a, LANES - s * S5_GROUP, 1))
        arrs = nxt
        s //= 2
    return arrs


def _s5_body(uf_ref, ub_ref, wa_ref, wo_ref, are_ref, aim_ref, yf_ref, yb_ref, *h_refs):
    @pl.when(pl.program_id(1) == 0)
    def _():
        for h_ref in h_refs:
            h_ref[...] = jnp.zeros_like(h_ref)

    ops = (wa_ref, wo_ref, are_ref, aim_ref)
    slab_f = _s5_dir(uf_ref, *ops, yf_ref, *h_refs[:2], rev=False)
    slab_b = _s5_dir(ub_ref, *ops, yb_ref, *h_refs[2:], rev=True)

    for k0 in range(0, S5_WIDTH // LANES, SLABS_IN_FLIGHT):
        _interleave(*[slab(k) for k in range(k0, k0 + SLABS_IN_FLIGHT) for slab in (slab_f, slab_b)])


def _s5_dir(u_ref, wa_ref, wo_ref, are_ref, aim_ref, y_ref, hre_ref, him_ref, *, rev):
    d = 1 if rev else 0
    rws = S5_ROWS
    row = _iota((rws, 1), 0)
    pos = rws - 1 - row if rev else row
    per_slab = LANES // S5_GROUP
    pairs = per_slab // 2

    def earlier(v, k):
        return pltpu.roll(v, rws - k if rev else k, 0)

    def slab(k):
        parts = [u_ref[0, k, pl.ds(t, rws, stride=S5_T), :] for t in range(S5_T)]
        h_all_re, h_all_im = hre_ref[k], him_ref[k]
        xs = _block_transpose(parts)
        yield
        first = []
        for q in range(pairs):
            x2 = jnp.concatenate([xs[2 * q], xs[2 * q + 1]], axis=1).astype(BF16)
            first.append(_dot(x2, wa_ref[d, k * pairs + q]))
            yield
        ys, new_re, new_im = [], [], []
        for q in range(pairs):
            pr = k * pairs + q
            res = first[q]
            y_intra, h_re, h_im = res[:, :2 * LANES], res[:, 2 * LANES:3 * LANES], res[:, 3 * LANES:]
            span = 1
            while span < SUB:
                idx = rws - span if rev else span - 1
                a_re, a_im = are_ref[d, pr, idx:idx + 1, :], aim_ref[d, pr, idx:idx + 1, :]
                s_re = jnp.where(pos % SUB >= span, earlier(h_re, span), 0.0)
                s_im = jnp.where(pos % SUB >= span, earlier(h_im, span), 0.0)
                h_re, h_im = h_re + (a_re * s_re - a_im * s_im), h_im + (a_re * s_im + a_im * s_re)
                span *= 2
            c_re, c_im = h_all_re[SUB * q:SUB * q + 1, :], h_all_im[SUB * q:SUB * q + 1, :]
            in_re, in_im = jnp.broadcast_to(c_re, (rws, LANES)), jnp.broadcast_to(c_im, (rws, LANES))
            first_block = slice(rws - SUB, rws) if rev else slice(0, SUB)
            t_re, t_im = are_ref[d, pr, first_block, :], aim_ref[d, pr, first_block, :]
            blocks_re, blocks_im = {}, {}
            for v in (range(rws // SUB - 1, -1, -1) if rev else range(rws // SUB)):
                blk = slice(SUB * v, SUB * (v + 1))
                b_re = h_re[blk] + (t_re * c_re - t_im * c_im)
                b_im = h_im[blk] + (t_re * c_im + t_im * c_re)
                blocks_re[v], blocks_im[v] = b_re, b_im
                edge = 0 if rev else SUB - 1
                c_re, c_im = b_re[edge:edge + 1, :], b_im[edge:edge + 1, :]
            h_re = jnp.concatenate([blocks_re[v] for v in range(rws // SUB)], axis=0)
            h_im = jnp.concatenate([blocks_im[v] for v in range(rws // SUB)], axis=0)
            p_re = jnp.where(pos == 0, in_re, earlier(h_re, 1))
            p_im = jnp.where(pos == 0, in_im, earlier(h_im, 1))
            new_re.append(jnp.broadcast_to(c_re, (SUB, LANES)))
            new_im.append(jnp.broadcast_to(c_im, (SUB, LANES)))
            y2 = y_intra + _dot(jnp.concatenate([p_re, p_im], axis=1).astype(BF16), wo_ref[d, pr])
            ys += [y2[:, :LANES], y2[:, LANES:]]
            yield
        hre_ref[k] = jnp.concatenate(new_re, axis=0)
        him_ref[k] = jnp.concatenate(new_im, axis=0)
        for t, out in enumerate(_block_transpose(ys)):
            y_ref[0, k, pl.ds(t, rws, stride=S5_T), :] = out

    return slab


def _s5_scan(u5, ops, *, seq):
    n_batch, slabs, total, _ = u5.shape
    n_lat_tiles = seq // TILE
    tok = lambda d: pl.BlockSpec((1, slabs, TILE, LANES), lambda b, i: (b, 0, _tile_index(d, i, n_lat_tiles), 0))
    full = lambda a: pl.BlockSpec(a.shape, lambda b, i: (0,) * a.ndim)
    pairs_per_slab = LANES // S5_GROUP // 2
    y_shape = jax.ShapeDtypeStruct((n_batch, slabs, total, LANES), F32)
    return pl.pallas_call(
        _s5_body,
        grid=(n_batch, n_lat_tiles + 1),
        in_specs=[tok(0), tok(1)] + [full(a) for a in ops],
        out_specs=[tok(0), tok(1)],
        out_shape=[y_shape, y_shape],
        scratch_shapes=[pltpu.VMEM((slabs, SUB * pairs_per_slab, LANES), F32)] * 4,
        compiler_params=_params(("arbitrary", "arbitrary")),
        name="s5_scan",
    )(u5, u5, *ops)


def _outproj_body(*refs, col_major, last, n_batch, n_lat_tiles):
    n_tok = 1 if last else 2
    lat_ref = refs[0]
    ctx_ref = None if last else refs[1]
    (mod_ref, yf_ref, yb_ref, z_ref, of_ref, ob_ref, gg_ref, sf_ref, sb_ref, u5_ref, sg_ref,
     snw_ref, gnw_ref, s5d_ref, gluw_ref, glub_ref, wout_ref, fnw_ref, o_ref) = refs[n_tok:]
    b, j = pl.program_id(0), pl.program_id(1)
    is_ctx = j == n_lat_tiles
    res = _load_tile(lat_ref, ctx_ref, col_major, is_ctx, jnp.minimum(j, n_lat_tiles - 1))
    gate = mod_ref[pl.ds(jnp.where(is_ctx, n_batch, b), 1), 2 * D_MODEL:3 * D_MODEL]

    def rms(v):
        return v * lax.rsqrt(jnp.mean(v * v, axis=-1, keepdims=True) + EPS)

    y = (yf_ref[0].astype(F32) + yb_ref[0].astype(F32)) * _silu(z_ref[0].astype(F32))
    half = SSD_WIDTH // 2
    ssd = jnp.concatenate([rms(y[:, :half]), rms(y[:, half:])], axis=1) * snw_ref[...]
    out = _dot(ssd.astype(BF16), wout_ref[0:SSD_WIDTH, :])

    o = of_ref[0].astype(F32) + ob_ref[0].astype(F32)
    gla = jnp.concatenate([rms(o[:, h * GLA_DV:(h + 1) * GLA_DV]) for h in range(GLA_HEADS)], axis=1)
    gla = gla * gnw_ref[...] * _silu(gg_ref[0].astype(F32))
    out = out + _dot(gla.astype(BF16), wout_ref[SSD_WIDTH:SSD_WIDTH + GLA_WIDTH, :])

    slabs = range(S5_WIDTH // LANES)
    y5 = jnp.concatenate([sf_ref[0, k] + sb_ref[0, k] for k in slabs], axis=1)
    u5 = jnp.concatenate([u5_ref[0, k] for k in slabs], axis=1)
    y5 = jax.nn.gelu(y5 + s5d_ref[...] * u5)
    pr = _dot(y5.astype(BF16), gluw_ref[...]) + glub_ref[...]
    s5 = pr[:, :S5_WIDTH] * jax.nn.sigmoid(pr[:, S5_WIDTH:]) * _silu(sg_ref[0].astype(F32))
    out = out + _dot(s5.astype(BF16), wout_ref[SSD_WIDTH + GLA_WIDTH:, :])

    h_new = res + gate * out
    if last:
        h_new = rms(h_new) * fnw_ref[...]
    if col_major:
        rows = o_ref.shape[1]
        base = _col_base(j, rows)
        for k in range(TILE // rows):
            o_ref[0, :, base + k, :] = h_new[k * rows:(k + 1) * rows, :]
    else:
        o_ref[0] = h_new


def _outproj(lat, ctx, ctx_block, mod_l, y_ssd, z, o_gla, gg, y_s5, u5, sg, ssd_norm_w, gla_norm_w, s5_d,
             glu_w, glu_b, w_out, final_norm_w, *, col_major, last, n_batch, seq, rows):
    n_lat_tiles = seq // TILE
    n_tiles = n_lat_tiles if last else n_lat_tiles + 1
    tok = lambda width: pl.BlockSpec((1, TILE, width), lambda b, j: (b, j, 0))
    both = lambda width: [tok(width), tok(width)]
    slabs = S5_WIDTH // LANES
    s5_tok = pl.BlockSpec((1, slabs, TILE, LANES), lambda b, j: (b, 0, j, 0))
    full = lambda a: pl.BlockSpec(a.shape, lambda b, j: (0,) * a.ndim)
    small = [ssd_norm_w, gla_norm_w, s5_d, glu_w, glu_b, w_out, final_norm_w]
    tokens = (lat,) if last else (lat, ctx)
    if col_major:
        per_block = COL_W // (TILE // rows)
        out_spec = pl.BlockSpec((1, rows, COL_W, D_MODEL), lambda b, j: (b, 0, j // per_block, 0))
        out_shape = jax.ShapeDtypeStruct((n_batch, rows, seq // rows, D_MODEL), F32)
    else:
        out_spec = tok(D_MODEL)
        out_shape = jax.ShapeDtypeStruct((n_batch, n_tiles * TILE, D_MODEL), F32)
    return pl.pallas_call(
        functools.partial(_outproj_body, col_major=col_major, last=last, n_batch=n_batch,
                          n_lat_tiles=n_lat_tiles),
        grid=(n_batch, n_tiles),
        in_specs=_token_specs(n_lat_tiles, rows, col_major, not last, ctx_block)
        + [full(mod_l)] + both(SSD_WIDTH) + [tok(SSD_WIDTH)] + both(GLA_WIDTH) + [tok(GLA_WIDTH)]
        + [s5_tok, s5_tok, s5_tok, tok(S5_WIDTH)]
        + [full(a) for a in small],
        out_specs=out_spec,
        out_shape=out_shape,
        compiler_params=_params(("arbitrary", "arbitrary")),
        name="outproj_col" if col_major else "outproj_row",
    )(*tokens, mod_l, *y_ssd, z, *o_gla, gg, *y_s5, u5, sg, *small)


def _pack_w_in(w):
    sizes = (1024, SSD_CONV_DIM, 32, 256, 256, 512, 512, 32, 512, 512)
    parts, start = [], 0
    for size in sizes:
        parts.append(w[:, start:start + size])
        start += size
    z, xbc, dt, q, k, v, gg, lr, u5, sg = parts
    pad = lambda a: jnp.pad(a, ((0, 0), (0, LANES - a.shape[1])))
    cols = [z, xbc, pad(jnp.concatenate([dt, lr], axis=1)), q, k, v, gg, u5, sg]
    return jnp.concatenate(cols, axis=1).astype(BF16)


def kernel(x, c, ctx, c_ctx, norm_w, mod_w, mod_b, w_in, w_out, ssd_conv_w, ssd_conv_b, ssd_a_log, ssd_dt_bias, ssd_d, ssd_norm_w, gla_w_lr, gla_b_lr, gla_norm_w, s5_lam_re, s5_lam_im, s5_log_step, s5_b_re, s5_b_im, s5_c_re, s5_c_im, s5_d, s5_glu_w, s5_glu_b, final_norm_w):
    n_batch, seq, d_model = x.shape
    depth = norm_w.shape[0]
    rows = seq // GRID_W
    assert d_model == D_MODEL and ctx.shape[1] == CTX_LEN == TILE
    assert seq % TILE == 0 and TILE % rows == 0
    assert depth == 2, "layer 0 row-major with context update, layer 1 column-major and last"
    n_lat_tiles = seq // TILE

    mod_rows = -(-(n_batch + 1) // 8) * 8
    cc = jnp.zeros((mod_rows, d_model), F32).at[:n_batch].set(c).at[n_batch].set(c_ctx)
    mod = _modulation(cc, mod_w, mod_b)

    lat, ctx_arr, ctx_block = x, ctx, 0
    for l in range(depth):
        col_major = l % 2 == 1
        last = l == depth - 1
        if col_major:
            lat_in = lat.reshape(n_batch, lat.shape[1] // GRID_W, GRID_W, d_model)
        else:
            lat_in = lat
        z, xc, small, qkv, gg, u5, sg = _inproj(
            lat_in, ctx_arr, ctx_block, mod[l], norm_w[l].reshape(1, -1), _pack_w_in(w_in[l]),
            ssd_conv_w[l], ssd_conv_b[l].reshape(1, -1),
            col_major=col_major, n_batch=n_batch, seq=seq, rows=rows)
        pad_h = lambda a: jnp.pad(a.reshape(1, -1), ((0, 0), (SMALL_DT, LANES - SMALL_DT - a.size)))
        y_ssd = _ssd_scan(xc, small, pad_h(ssd_dt_bias[l]), pad_h(ssd_a_log[l]),
                          jnp.repeat(ssd_d[l], SSD_HEADDIM).reshape(1, -1), seq=seq)

        w_lr = jnp.stack([jnp.pad(gla_w_lr[l, d], ((SMALL_LR + GLA_RANK * d, LANES - SMALL_LR - GLA_RANK * (d + 1)), (0, 0)))
                          for d in (0, 1)])
        o_gla = _gla_scan(qkv, small, w_lr, gla_b_lr[l].reshape(2, 1, -1), seq=seq)

        ops = _s5_prep(s5_lam_re[l], s5_lam_im[l], s5_log_step[l], s5_b_re[l], s5_b_im[l],
                       s5_c_re[l], s5_c_im[l])
        ops = _s5_pair_ops(*ops)
        y_s5 = _s5_scan(u5, ops, seq=seq)

        out = _outproj(
            lat_in, ctx_arr, ctx_block, mod[l], y_ssd, z, o_gla, gg, y_s5, u5, sg,
            ssd_norm_w[l].reshape(1, -1), jnp.tile(gla_norm_w[l], GLA_HEADS).reshape(1, -1),
            s5_d[l].reshape(1, -1), s5_glu_w[l].astype(BF16), s5_glu_b[l].reshape(1, -1),
            w_out[l].astype(BF16), final_norm_w.reshape(1, -1),
            col_major=col_major, last=last, n_batch=n_batch, seq=seq, rows=rows)
        if last:
            return out.reshape(n_batch, seq, d_model)
        lat, ctx_arr, ctx_block = out, out, n_lat_tiles
```

```python
import functools

import jax
import jax.numpy as jnp
from jax import lax
from jax.experimental import pallas as pl
from jax.experimental.pallas import tpu as pltpu

F32, BF16 = jnp.float32, jnp.bfloat16
STORE = BF16
HIGHEST = lax.Precision.HIGHEST
EPS = 1e-6

D_MODEL = 1024
GRID_W = 64
CTX_LEN = 256
TILE = 256
SCAN_BATCH = 2
COL_W = 8
LANES = 128
SUB = 8
CONV_COLS = 256
VMEM_LIMIT = 56 * 1024 * 1024

SSD_WIDTH, SSD_HEADS, SSD_HEADDIM, SSD_STATE, SSD_CHUNK = 1024, 16, 64, 128, 128
SSD_CONV_DIM, SSD_CONV_K = 1536, 5
GLA_HEADS, GLA_DK, GLA_DV, GLA_RANK, GLA_TAU, GLA_CHUNK = 4, 64, 128, 16, 16.0, 64
GLA_WIDTH = GLA_HEADS * GLA_DV
S5_WIDTH, S5_GROUP, S5_GROUPS, S5_STATE = 512, 16, 32, 64
S5_T = LANES // S5_GROUP
MIX_WIDTH = SSD_WIDTH + GLA_WIDTH + S5_WIDTH

C_Z, C_XBC, C_SMALL = 0, 1024, 2560
C_QKV, C_GG, C_U5, C_SG, C_END = 2688, 3712, 4224, 4736, 5248
SMALL_DT, SMALL_LR = 0, 32


def _params(semantics):
    return pltpu.CompilerParams(dimension_semantics=semantics, vmem_limit_bytes=VMEM_LIMIT)


def _silu(v):
    return v * jax.nn.sigmoid(v)


def _dot(a, b):
    return jnp.dot(a, b, preferred_element_type=F32)


def _dot_nt(a, b):
    return lax.dot_general(a, b, (((1,), (1,)), ((), ())), preferred_element_type=F32)


def _dot_exact_lhs(m_bf16, v, terms=3):
    out, rest = None, v
    for _ in range(terms):
        piece = rest.astype(BF16)
        rest = rest - piece.astype(F32)
        out = _dot(m_bf16, piece) if out is None else out + _dot(m_bf16, piece)
    return out


def _iota(shape, dim):
    return lax.broadcasted_iota(jnp.int32, shape, dim)


def _interleave(*stages):
    active = list(stages)
    while active:
        for gen in list(active):
            try:
                next(gen)
            except StopIteration:
                active.remove(gen)


def _mod_body(cc_ref, w_ref, b_ref, o_ref):
    s = _silu(cc_ref[...])
    o_ref[0] = jnp.dot(s, w_ref[0], precision=HIGHEST, preferred_element_type=F32) + b_ref[0]


def _modulation(cc, mod_w, mod_b):
    depth, d, three_d = mod_w.shape
    rows = cc.shape[0]
    return pl.pallas_call(
        _mod_body,
        grid=(depth, three_d // d),
        in_specs=[pl.BlockSpec((rows, d), lambda l, n: (0, 0)),
                  pl.BlockSpec((1, d, d), lambda l, n: (l, 0, n)),
                  pl.BlockSpec((1, 1, d), lambda l, n: (l, 0, n))],
        out_specs=pl.BlockSpec((1, rows, d), lambda l, n: (l, 0, n)),
        out_shape=jax.ShapeDtypeStruct((depth, rows, three_d), F32),
        compiler_params=_params(("arbitrary", "arbitrary")),
        name="modulation",
    )(cc, mod_w, mod_b.reshape(depth, 1, three_d))


def _col_base(j, rows):
    cols = TILE // rows
    return (j % (COL_W // cols)) * cols


def _load_tile(lat_ref, ctx_ref, col_major, is_ctx, j):
    if col_major:
        rows = lat_ref.shape[1]
        base = _col_base(j, rows)
        lat = jnp.concatenate([lat_ref[0, :, base + k, :] for k in range(TILE // rows)], axis=0)
    else:
        lat = lat_ref[0]
    if ctx_ref is None:
        return lat
    return jnp.where(is_ctx, ctx_ref[0], lat)


def _token_specs(n_lat_tiles, rows, col_major, with_ctx, ctx_block):
    last = n_lat_tiles - 1
    if col_major:
        per_block = COL_W // (TILE // rows)
        lat = pl.BlockSpec((1, rows, COL_W, D_MODEL), lambda b, j: (b, 0, jnp.minimum(j, last) // per_block, 0))
    else:
        lat = pl.BlockSpec((1, TILE, D_MODEL), lambda b, j: (b, jnp.minimum(j, last), 0))
    if not with_ctx:
        return [lat]
    return [lat, pl.BlockSpec((1, TILE, D_MODEL), lambda b, j: (b, ctx_block, 0))]


def _halo_specs(n_lat_tiles, rows, col_major, seq):
    last = n_lat_tiles - 1
    if col_major:
        cols = TILE // rows
        tile = lambda j: jnp.minimum(j, last)
        prev = pl.BlockSpec((1, SUB, COL_W, D_MODEL),
                            lambda b, j: (b, rows // SUB - 1, jnp.maximum(tile(j) * cols - 1, 0) // COL_W, 0))
        nxt = pl.BlockSpec((1, SUB, COL_W, D_MODEL),
                           lambda b, j: (b, 0, jnp.minimum((tile(j) + 1) * cols, GRID_W - 1) // COL_W, 0))
    else:
        per_tile, n_blk = TILE // SUB, seq // SUB
        tile = lambda j: jnp.minimum(j, last)
        prev = pl.BlockSpec((1, SUB, D_MODEL), lambda b, j: (b, jnp.maximum(tile(j) * per_tile - 1, 0), 0))
        nxt = pl.BlockSpec((1, SUB, D_MODEL), lambda b, j: (b, jnp.minimum((tile(j) + 1) * per_tile, n_blk - 1), 0))
    return [prev, nxt]


def _load_halo(prev_ref, next_ref, col_major, j, rows):
    if col_major:
        cols = TILE // rows
        prev = prev_ref[0, :, jnp.maximum(j * cols - 1, 0) % COL_W, :]
        nxt = next_ref[0, :, jnp.minimum((j + 1) * cols, GRID_W - 1) % COL_W, :]
    else:
        prev, nxt = prev_ref[0], next_ref[0]
    return jnp.concatenate([prev, nxt], axis=0)


def _inproj_body(*refs, col_major, n_batch, n_lat_tiles):
    lat_ref, ctx_ref, prev_ref, next_ref, mod_ref, nw_ref, w_ref, cw_ref, cb_ref = refs[:9]
    z_ref, xc_ref, small_ref, qkv_ref, gg_ref, u5_ref, sg_ref = refs[9:]
    b, j = pl.program_id(0), pl.program_id(1)
    is_ctx = j == n_lat_tiles
    jj = jnp.minimum(j, n_lat_tiles - 1)
    x = _load_tile(lat_ref, ctx_ref, col_major, is_ctx, jj)
    m = mod_ref[pl.ds(jnp.where(is_ctx, n_batch, b), 1), :]
    shift, scale = m[:, :D_MODEL], m[:, D_MODEL:2 * D_MODEL]

    def modulated(v):
        v = v * lax.rsqrt(jnp.mean(v * v, axis=-1, keepdims=True) + EPS) * nw_ref[...]
        return (v * (1.0 + scale) + shift).astype(BF16)

    u = modulated(x)

    u_halo = modulated(_load_halo(prev_ref, next_ref, col_major, jj, lat_ref.shape[1]))
    has_prev = jnp.logical_and(jnp.logical_not(is_ctx), j > 0)
    has_next = jnp.logical_and(jnp.logical_not(is_ctx), j < n_lat_tiles - 1)

    z_ref[0] = _dot(u, w_ref[:, C_Z:C_XBC]).astype(z_ref.dtype)
    for c0 in range(0, SSD_CONV_DIM, CONV_COLS):
        cols = slice(C_XBC + c0, C_XBC + c0 + CONV_COLS)
        halo = _dot(u_halo, w_ref[:, cols])
        ext = jnp.concatenate([jnp.where(has_prev, halo[:SUB], 0.0), _dot(u, w_ref[:, cols]),
                               jnp.where(has_next, halo[SUB:], 0.0)], axis=0)
        acc = jnp.broadcast_to(cb_ref[:, c0:c0 + CONV_COLS], (TILE, CONV_COLS))
        for k in range(SSD_CONV_K):
            off = SUB + k - SSD_CONV_K // 2
            acc = acc + cw_ref[k:k + 1, c0:c0 + CONV_COLS] * ext[off:off + TILE, :]
        xc_ref[0, :, c0:c0 + CONV_COLS] = _silu(acc).astype(xc_ref.dtype)
    small_ref[0] = _dot(u, w_ref[:, C_SMALL:C_QKV])
    qkv_ref[0] = _dot(u, w_ref[:, C_QKV:C_GG]).astype(qkv_ref.dtype)
    gg_ref[0] = _dot(u, w_ref[:, C_GG:C_U5]).astype(gg_ref.dtype)
    u5 = _dot(u, w_ref[:, C_U5:C_SG])
    for k in range(S5_WIDTH // LANES):
        u5_ref[0, k] = u5[:, k * LANES:(k + 1) * LANES]
    sg_ref[0] = _dot(u, w_ref[:, C_SG:C_END]).astype(sg_ref.dtype)


def _inproj(lat, ctx, ctx_block, mod_l, norm_w, w_packed, conv_w, conv_b, *, col_major, n_batch, seq, rows):
    n_lat_tiles = seq // TILE
    lt = seq + CTX_LEN
    tok = lambda width: pl.BlockSpec((1, TILE, width), lambda b, j: (b, j, 0))
    full = lambda a: pl.BlockSpec(a.shape, lambda b, j: (0,) * a.ndim)
    shape = lambda width, dtype=STORE: jax.ShapeDtypeStruct((n_batch, lt, width), dtype)
    return pl.pallas_call(
        functools.partial(_inproj_body, col_major=col_major, n_batch=n_batch, n_lat_tiles=n_lat_tiles),
        grid=(n_batch, n_lat_tiles + 1),
        in_specs=_token_specs(n_lat_tiles, rows, col_major, True, ctx_block)
        + _halo_specs(n_lat_tiles, rows, col_major, seq)
        + [full(mod_l), full(norm_w), full(w_packed), full(conv_w), full(conv_b)],
        out_specs=[tok(1024), tok(SSD_CONV_DIM), tok(LANES), tok(1024), tok(512),
                   pl.BlockSpec((1, S5_WIDTH // LANES, TILE, LANES), lambda b, j: (b, 0, j, 0)),
                   tok(512)],
        out_shape=[shape(1024), shape(SSD_CONV_DIM), shape(LANES, F32), shape(1024), shape(512),
                   jax.ShapeDtypeStruct((n_batch, S5_WIDTH // LANES, lt, LANES), F32), shape(512)],
        compiler_params=_params(("arbitrary", "arbitrary")),
        name="inproj_col" if col_major else "inproj_row",
    )(lat, ctx, lat, lat, mod_l, norm_w, w_packed, conv_w, conv_b)


def _ssd_body(xcf_ref, dtf_ref, xcb_ref, dtb_ref, bias_ref, alog_ref, dsk_ref, yf_ref, yb_ref, stf_ref, stb_ref):
    @pl.when(pl.program_id(1) == 0)
    def _():
        stf_ref[...] = jnp.zeros_like(stf_ref)
        stb_ref[...] = jnp.zeros_like(stb_ref)

    chains = []
    for bi in range(xcf_ref.shape[0]):
        chains.append(_ssd_dir(xcf_ref, dtf_ref, bias_ref, alog_ref, dsk_ref, yf_ref, stf_ref, bi, rev=False))
        chains.append(_ssd_dir(xcb_ref, dtb_ref, bias_ref, alog_ref, dsk_ref, yb_ref, stb_ref, bi, rev=True))
    _interleave(*chains)


def _ssd_dir(xc_ref, dt_ref, bias_ref, alog_ref, dsk_ref, y_ref, st_ref, bi, *, rev):
    t = SSD_CHUNK
    d = 1 if rev else 0
    off = SMALL_DT + SSD_HEADS * d
    dt = jax.nn.softplus(dt_ref[bi] + bias_ref[...])
    da = dt * -jnp.exp(alog_ref[...])
    row, col = _iota((t, t), 0), _iota((t, t), 1)
    tri = row <= col if rev else row >= col
    p = _dot_exact_lhs(jnp.where(tri, 1.0, 0.0).astype(BF16), da)
    total = jnp.sum(da, axis=0, keepdims=True)
    e_tot = jnp.exp(total)
    p_t = p.T
    dt_t = dt.T
    w_t = (dt * jnp.exp(total - p)).T
    lo = _iota((1, LANES), 1) < SSD_HEADDIM
    hi = jnp.logical_not(lo)
    pairs_per_group = SSD_HEADS // 2 // 2
    yield
    for g in range(2):
        bg = xc_ref[bi, :, SSD_WIDTH + g * SSD_STATE:SSD_WIDTH + (g + 1) * SSD_STATE]
        cg = xc_ref[bi, :, SSD_WIDTH + (2 + g) * SSD_STATE:SSD_WIDTH + (3 + g) * SSD_STATE]
        cg_b = cg.astype(BF16)
        cb = _dot_nt(cg_b, bg.astype(BF16))
        bg_t = bg.astype(F32).T
        for q in range(pairs_per_group):
            hp = g * pairs_per_group + q
            h0, h1 = off + 2 * hp, off + 2 * hp + 1
            pb0 = jnp.broadcast_to(p[:, h0:h0 + 1], (t, LANES))
            pb1 = jnp.broadcast_to(p[:, h1:h1 + 1], (t, LANES))
            xp = xc_ref[bi, :, hp * LANES:(hp + 1) * LANES]
            st = st_ref[bi, hp]
            acc = jnp.exp(jnp.where(lo, pb0, pb1)) * _dot(cg_b, st.astype(BF16))
            if not rev:
                acc = acc + dsk_ref[:, hp * LANES:(hp + 1) * LANES] * xp.astype(F32)
            st = jnp.where(lo, e_tot[:, h0:h0 + 1], e_tot[:, h1:h1 + 1]) * st
            for h, pb, sel in ((h0, pb0, lo), (h1, pb1, hi)):
                x_h = jnp.where(sel, xp, jnp.zeros_like(xp)).astype(BF16)
                decay = jnp.where(tri, jnp.exp(pb - p_t[h:h + 1, :]), 0.0)
                acc = acc + _dot((cb * decay * dt_t[h:h + 1, :]).astype(BF16), x_h)
                st = st + _dot((bg_t * w_t[h:h + 1, :]).astype(BF16), x_h)
            y_ref[bi, :, hp * LANES:(hp + 1) * LANES] = acc.astype(y_ref.dtype)
            st_ref[bi, hp] = st
            yield


def _ssd_scan(xc, dt, dt_bias, a_log, dsk, *, seq):
    n_batch, total, _ = xc.shape
    nc = total // SSD_CHUNK
    lat_c = seq // SSD_CHUNK
    ctx_c = nc - lat_c
    fwd = lambda i: jnp.where(i < ctx_c, lat_c + i, i - ctx_c)
    bwd = lambda i: nc - 1 - i
    full = lambda a: pl.BlockSpec(a.shape, lambda b, i: (0,) * a.ndim)
    nb = SCAN_BATCH if n_batch % SCAN_BATCH == 0 else 1
    xc_spec = lambda chunk: pl.BlockSpec((nb, SSD_CHUNK, SSD_CONV_DIM), lambda b, i: (b, chunk(i), 0))
    dt_spec = lambda chunk: pl.BlockSpec((nb, SSD_CHUNK, LANES), lambda b, i: (b, chunk(i), 0))
    y_spec = lambda chunk: pl.BlockSpec((nb, SSD_CHUNK, SSD_WIDTH), lambda b, i: (b, chunk(i), 0))
    y_shape = jax.ShapeDtypeStruct((n_batch, total, SSD_WIDTH), STORE)
    return pl.pallas_call(
        _ssd_body,
        grid=(n_batch // nb, nc),
        in_specs=[xc_spec(fwd), dt_spec(fwd), xc_spec(bwd), dt_spec(bwd),
                  full(dt_bias), full(a_log), full(dsk)],
        out_specs=[y_spec(fwd), y_spec(bwd)],
        out_shape=[y_shape, y_shape],
        scratch_shapes=[pltpu.VMEM((nb, SSD_HEADS // 2, SSD_STATE, LANES), F32)] * 2,
        compiler_params=_params(("arbitrary", "arbitrary")),
        name="ssd_scan",
    )(xc, dt, xc, dt, dt_bias, a_log, dsk)


def _tile_index(d, i, n_lat_tiles):
    return jnp.where(i == 0, n_lat_tiles, jnp.where(d == 0, i - 1, n_lat_tiles - i))


def _gla_body(qkvf_ref, lrf_ref, qkvb_ref, lrb_ref, wlr_ref, blr_ref, of_ref, ob_ref, sf_ref, sb_ref):
    @pl.when(pl.program_id(1) == 0)
    def _():
        sf_ref[...] = jnp.zeros_like(sf_ref)
        sb_ref[...] = jnp.zeros_like(sb_ref)

    chains = []
    for bi in range(qkvf_ref.shape[0]):
        chains.append(_gla_dir(qkvf_ref, lrf_ref, wlr_ref, blr_ref, of_ref, sf_ref, bi, rev=False))
        chains.append(_gla_dir(qkvb_ref, lrb_ref, wlr_ref, blr_ref, ob_ref, sb_ref, bi, rev=True))
    _interleave(*chains)


def _gla_dir(qkv_ref, lr_ref, wlr_ref, blr_ref, o_ref, s_ref, bi, *, rev):
    d = 1 if rev else 0
    t = GLA_CHUNK
    n_chunks = TILE // t
    kw = GLA_HEADS * GLA_DK
    row, col = _iota((TILE, TILE), 0), _iota((TILE, TILE), 1)
    same = row // t == col // t
    tri = jnp.where(same, row - col, 1 if rev else -1)
    tri = tri <= 0 if rev else tri >= 0
    lo = _iota((1, LANES), 1) < GLA_DK
    hi = jnp.logical_not(lo)
    eye = _iota((LANES, LANES), 0) == _iota((LANES, LANES), 1)
    row_lo = _iota((LANES, 1), 0) < GLA_DK

    logit = jnp.dot(lr_ref[bi], wlr_ref[d], precision=HIGHEST, preferred_element_type=F32) + blr_ref[d]
    yield
    g = jax.nn.log_sigmoid(logit) / GLA_TAU
    p = _dot_exact_lhs(jnp.where(tri, 1.0, 0.0).astype(BF16), g, terms=2)
    total = jnp.concatenate(
        [jnp.broadcast_to(p[(c * t if rev else (c + 1) * t - 1):(c * t + 1 if rev else (c + 1) * t), :], (t, kw))
         for c in range(n_chunks)], axis=0)
    yield
    q = qkv_ref[bi, :, 0:kw].astype(F32) * GLA_DK ** -0.5
    k = qkv_ref[bi, :, kw:2 * kw].astype(F32)
    qe = q * jnp.exp(p)
    ke = k * jnp.exp(-p)
    kd = k * jnp.exp(total - p)
    e_tot = jnp.exp(total)
    order = range(n_chunks - 1, -1, -1) if rev else range(n_chunks)
    pairs = range(GLA_HEADS // 2)
    lanes = lambda pr: slice(pr * LANES, (pr + 1) * LANES)
    qms, intra, vs = {}, {}, {}
    for pr in pairs:
        ke_p = ke[:, lanes(pr)].astype(BF16)
        for e, sel in ((0, lo), (1, hi)):
            h = 2 * pr + e
            qms[h] = jnp.where(sel, qe[:, lanes(pr)], 0.0).astype(BF16)
            vs[h] = qkv_ref[bi, :, 2 * kw + h * GLA_DV:2 * kw + (h + 1) * GLA_DV].astype(BF16)
            attn = jnp.where(tri, _dot_nt(qms[h], ke_p), 0.0).astype(BF16)
            yield
            intra[h] = _dot(attn, vs[h])
            yield
    upd, dec = {}, {}
    for pr in pairs:
        for c in order:
            rows = slice(c * t, (c + 1) * t)
            kd_t = kd[rows, lanes(pr)].T.astype(BF16)
            upd[pr, c] = jnp.where(row_lo, _dot(kd_t, vs[2 * pr][rows]), _dot(kd_t, vs[2 * pr + 1][rows]))
            dec[pr, c] = jnp.sum(jnp.where(eye, e_tot[c * t:c * t + 1, lanes(pr)], 0.0), axis=1, keepdims=True)
        yield
    for pr in pairs:
        s = s_ref[bi, pr]
        for c in order:
            rows = slice(c * t, (c + 1) * t)
            s_b = s.astype(BF16)
            for h in (2 * pr, 2 * pr + 1):
                o_ref[bi, rows, h * GLA_DV:(h + 1) * GLA_DV] = (intra[h][rows] + _dot(qms[h][rows], s_b)).astype(o_ref.dtype)
            s = dec[pr, c] * s + upd[pr, c]
            yield
        s_ref[bi, pr] = s


def _gla_scan(qkv, lr, w_lr, b_lr, *, seq):
    n_batch, total, _ = qkv.shape
    n_lat_tiles = seq // TILE
    tile = lambda d: (lambda i: _tile_index(d, i, n_lat_tiles))
    full = lambda a: pl.BlockSpec(a.shape, lambda b, i: (0,) * a.ndim)
    nb = SCAN_BATCH if n_batch % SCAN_BATCH == 0 else 1
    qkv_spec = lambda d: pl.BlockSpec((nb, TILE, 1024), lambda b, i: (b, tile(d)(i), 0))
    lr_spec = lambda d: pl.BlockSpec((nb, TILE, LANES), lambda b, i: (b, tile(d)(i), 0))
    o_spec = lambda d: pl.BlockSpec((nb, TILE, GLA_WIDTH), lambda b, i: (b, tile(d)(i), 0))
    o_shape = jax.ShapeDtypeStruct((n_batch, total, GLA_WIDTH), STORE)
    return pl.pallas_call(
        _gla_body,
        grid=(n_batch // nb, n_lat_tiles + 1),
        in_specs=[qkv_spec(0), lr_spec(0), qkv_spec(1), lr_spec(1), full(w_lr), full(b_lr)],
        out_specs=[o_spec(0), o_spec(1)],
        out_shape=[o_shape, o_shape],
        scratch_shapes=[pltpu.VMEM((nb, GLA_HEADS // 2, LANES, GLA_DV), F32)] * 2,
        compiler_params=_params(("arbitrary", "arbitrary")),
        name="gla_scan",
    )(qkv, lr, qkv, lr, w_lr, b_lr)


S5_ROWS = TILE // S5_T
SLABS_IN_FLIGHT = 4


def _s5_prep_body(ls_ref, lrr_ref, lir_ref, lrc_ref, lic_ref, p1_ref, p2_ref, q1_ref, q2_ref,
                  g_ref, win_ref, wout_ref, ar_ref, ai_ref):
    d = pl.program_id(0)
    fwd = d == 0
    n = LANES
    step = jnp.exp(ls_ref[0, 0])
    lo = _iota((1, n), 1) < S5_STATE
    top = _iota((n, 1), 0) < S5_STATE

    lre, lim = lrr_ref[0, 0], lir_ref[0, 0]
    lr, ang = lre * step, lim * step
    er = jnp.exp(lr)
    nr, ni = er * jnp.cos(ang) - 1.0, er * jnp.sin(ang)
    den = lre * lre + lim * lim
    cr, ci = (nr * lre + ni * lim) / den, (ni * lre - nr * lim) / den
    p1, p2 = p1_ref[0], p2_ref[0]
    bbr, bbi = cr * p1 - ci * p2, cr * p2 + ci * p1
    s_idx = _iota((n, 1), 0) // S5_GROUP
    pos_s = jnp.where(fwd, s_idx, S5_T - 1 - s_idx)
    xs = (S5_T - 1 - pos_s).astype(F32)
    e = jnp.exp(xs * lr)
    l_re, l_im = e * jnp.cos(xs * ang), e * jnp.sin(xs * ang)
    w_in = jnp.where(lo, bbr * l_re - bbi * l_im, bbr * l_im + bbi * l_re)
    win_ref[0, 0] = w_in.astype(BF16)

    lrc, angc = lrc_ref[0, 0] * step, lic_ref[0, 0] * step
    q1, q2 = q1_ref[0, 0], q2_ref[0, 0]
    t_idx = _iota((1, n), 1) // S5_GROUP
    pos_t = jnp.where(fwd, t_idx, S5_T - 1 - t_idx)

    def out_op(xt):
        ex = jnp.exp(lrc * xt)
        o_re, o_im = ex * jnp.cos(angc * xt), ex * jnp.sin(angc * xt)
        return jnp.where(top, q1 * o_re - q2 * o_im, -(q1 * o_im + q2 * o_re))

    wout_ref[0, 0] = out_op((pos_t + 1).astype(F32)).astype(BF16)
    g = jnp.dot(w_in, out_op((pos_t - (S5_T - 1)).astype(F32)), precision=HIGHEST,
                preferred_element_type=F32)
    g_ref[0, 0] = jnp.where(pos_t >= pos_s, g, 0.0).astype(BF16)

    r_idx = _iota((S5_ROWS, 1), 0)
    xe = (S5_T * (jnp.where(fwd, r_idx, S5_ROWS - 1 - r_idx) + 1)).astype(F32)
    ee = jnp.exp(xe * lr)
    a_im = ee * jnp.sin(xe * ang)
    ar_ref[0, 0] = ee * jnp.cos(xe * ang)
    ai_ref[0, 0] = jnp.where(lo, -a_im, a_im)


def _s5_prep(lam_re, lam_im, log_step, b_re, b_im, c_re, c_im):
    n = LANES
    dup_l = lambda a: jnp.concatenate([a, a], axis=-1)
    dup_r = lambda a: jnp.concatenate([a, a], axis=-2)
    lrr, lir = dup_l(lam_re)[:, :, None, :], dup_l(lam_im)[:, :, None, :]
    lrc, lic = lrr.reshape(2, S5_GROUPS, n, 1), lir.reshape(2, S5_GROUPS, n, 1)
    ls = log_step.reshape(2, S5_GROUPS, 1, 1)
    tile_rows = lambda a: dup_l(jnp.tile(jnp.swapaxes(a, -1, -2), (1, S5_T, 1)))
    p1, p2 = tile_rows(b_re), tile_rows(b_im)
    tile_cols = lambda a: dup_r(jnp.tile(jnp.swapaxes(a, -1, -2), (1, 1, 1, S5_T)))
    q1, q2 = tile_cols(c_re), tile_cols(c_im)
    dg = lambda *blk: pl.BlockSpec((1, 1) + blk, lambda d, g: (d, g, 0, 0))
    gg = lambda *blk: pl.BlockSpec((1,) + blk, lambda d, g: (g, 0, 0))
    out = lambda rows, dtype: jax.ShapeDtypeStruct((2, S5_GROUPS, rows, n), dtype)
    return pl.pallas_call(
        _s5_prep_body,
        grid=(2, S5_GROUPS),
        in_specs=[dg(1, 1), dg(1, n), dg(1, n), dg(n, 1), dg(n, 1), gg(n, n), gg(n, n), dg(n, n), dg(n, n)],
        out_specs=[dg(n, n), dg(n, n), dg(n, n), dg(S5_ROWS, n), dg(S5_ROWS, n)],
        out_shape=[out(n, BF16), out(n, BF16), out(n, BF16), out(S5_ROWS, F32), out(S5_ROWS, F32)],
        compiler_params=_params(("arbitrary", "arbitrary")),
        name="s5_prep",
    )(ls, lrr, lir, lrc, lic, p1, p2, q1, q2)


def _s5_pair_ops(g, win, wout, ar, ai):
    hs = S5_STATE
    cat = jnp.concatenate
    pair = lambda a: (a[:, 0::2], a[:, 1::2])
    g0, g1 = pair(g)
    i0, i1 = pair(win)
    o0, o1 = pair(wout)
    z = jnp.zeros_like(g0)
    zh, zr = z[..., :hs], z[..., :hs, :]
    wa = cat([cat([g0, z, i0[..., :hs], zh, i0[..., hs:], zh], -1),
              cat([z, g1, zh, i1[..., :hs], zh, i1[..., hs:]], -1)], -2)
    wo = cat([cat([o0[..., :hs, :], zr], -1), cat([zr, o1[..., :hs, :]], -1),
              cat([o0[..., hs:, :], zr], -1), cat([zr, o1[..., hs:, :]], -1)], -2)
    a0, a1 = pair(ar)
    b0, b1 = pair(ai)
    return wa, wo, cat([a0[..., :hs], a1[..., :hs]], -1), cat([b0[..., hs:], b1[..., hs:]], -1)


def _block_transpose(arrs):
    n = len(arrs)
    block = _iota((1, LANES), 1) // S5_GROUP
    arrs = list(arrs)
    s = n // 2
    while s >= 1:
        upper = (block // s) % 2 == 1
        nxt = list(arrs)
        for i in range(n):
            if (i // s) % 2 == 0:
                a, b = arrs[i], arrs[i + s]
                nxt[i] = jnp.where(upper, pltpu.roll(b, s * S5_GROUP, 1), a)
                nxt[i + s] = jnp.where(upper, b, pltpu.roll(a, LANES - s * S5_GROUP, 1))
        arrs = nxt
        s //= 2
    return arrs


def _s5_body(uf_ref, ub_ref, wa_ref, wo_ref, are_ref, aim_ref, yf_ref, yb_ref, *h_refs):
    @pl.when(pl.program_id(1) == 0)
    def _():
        for h_ref in h_refs:
            h_ref[...] = jnp.zeros_like(h_ref)

    ops = (wa_ref, wo_ref, are_ref, aim_ref)
    slab_f = _s5_dir(uf_ref, *ops, yf_ref, *h_refs[:2], rev=False)
    slab_b = _s5_dir(ub_ref, *ops, yb_ref, *h_refs[2:], rev=True)

    for k0 in range(0, S5_WIDTH // LANES, SLABS_IN_FLIGHT):
        _interleave(*[slab(k) for k in range(k0, k0 + SLABS_IN_FLIGHT) for slab in (slab_f, slab_b)])


def _s5_dir(u_ref, wa_ref, wo_ref, are_ref, aim_ref, y_ref, hre_ref, him_ref, *, rev):
    d = 1 if rev else 0
    rws = S5_ROWS
    row = _iota((rws, 1), 0)
    pos = rws - 1 - row if rev else row
    per_slab = LANES // S5_GROUP
    pairs = per_slab // 2

    def earlier(v, k):
        return pltpu.roll(v, rws - k if rev else k, 0)

    def slab(k):
        parts = [u_ref[0, k, pl.ds(t, rws, stride=S5_T), :] for t in range(S5_T)]
        h_all_re, h_all_im = hre_ref[k], him_ref[k]
        xs = _block_transpose(parts)
        yield
        first = []
        for q in range(pairs):
            x2 = jnp.concatenate([xs[2 * q], xs[2 * q + 1]], axis=1).astype(BF16)
            first.append(_dot(x2, wa_ref[d, k * pairs + q]))
            yield
        ys, new_re, new_im = [], [], []
        for q in range(pairs):
            pr = k * pairs + q
            res = first[q]
            y_intra, h_re, h_im = res[:, :2 * LANES], res[:, 2 * LANES:3 * LANES], res[:, 3 * LANES:]
            span = 1
            while span < SUB:
                idx = rws - span if rev else span - 1
                a_re, a_im = are_ref[d, pr, idx:idx + 1, :], aim_ref[d, pr, idx:idx + 1, :]
                s_re = jnp.where(pos % SUB >= span, earlier(h_re, span), 0.0)
                s_im = jnp.where(pos % SUB >= span, earlier(h_im, span), 0.0)
                h_re, h_im = h_re + (a_re * s_re - a_im * s_im), h_im + (a_re * s_im + a_im * s_re)
                span *= 2
            c_re, c_im = h_all_re[SUB * q:SUB * q + 1, :], h_all_im[SUB * q:SUB * q + 1, :]
            in_re, in_im = jnp.broadcast_to(c_re, (rws, LANES)), jnp.broadcast_to(c_im, (rws, LANES))
            first_block = slice(rws - SUB, rws) if rev else slice(0, SUB)
            t_re, t_im = are_ref[d, pr, first_block, :], aim_ref[d, pr, first_block, :]
            blocks_re, blocks_im = {}, {}
            for v in (range(rws // SUB - 1, -1, -1) if rev else range(rws // SUB)):
                blk = slice(SUB * v, SUB * (v + 1))
                b_re = h_re[blk] + (t_re * c_re - t_im * c_im)
                b_im = h_im[blk] + (t_re * c_im + t_im * c_re)
                blocks_re[v], blocks_im[v] = b_re, b_im
                edge = 0 if rev else SUB - 1
                c_re, c_im = b_re[edge:edge + 1, :], b_im[edge:edge + 1, :]
            h_re = jnp.concatenate([blocks_re[v] for v in range(rws // SUB)], axis=0)
            h_im = jnp.concatenate([blocks_im[v] for v in range(rws // SUB)], axis=0)
            p_re = jnp.where(pos == 0, in_re, earlier(h_re, 1))
            p_im = jnp.where(pos == 0, in_im, earlier(h_im, 1))
            new_re.append(jnp.broadcast_to(c_re, (SUB, LANES)))
            new_im.append(jnp.broadcast_to(c_im, (SUB, LANES)))
            y2 = y_intra + _dot(jnp.concatenate([p_re, p_im], axis=1).astype(BF16), wo_ref[d, pr])
            ys += [y2[:, :LANES], y2[:, LANES:]]
            yield
        hre_ref[k] = jnp.concatenate(new_re, axis=0)
        him_ref[k] = jnp.concatenate(new_im, axis=0)
        for t, out in enumerate(_block_transpose(ys)):
            y_ref[0, k, pl.ds(t, rws, stride=S5_T), :] = out

    return slab


def _s5_scan(u5, ops, *, seq):
    n_batch, slabs, total, _ = u5.shape
    n_lat_tiles = seq // TILE
    tok = lambda d: pl.BlockSpec((1, slabs, TILE, LANES), lambda b, i: (b, 0, _tile_index(d, i, n_lat_tiles), 0))
    full = lambda a: pl.BlockSpec(a.shape, lambda b, i: (0,) * a.ndim)
    pairs_per_slab = LANES // S5_GROUP // 2
    y_shape = jax.ShapeDtypeStruct((n_batch, slabs, total, LANES), F32)
    return pl.pallas_call(
        _s5_body,
        grid=(n_batch, n_lat_tiles + 1),
        in_specs=[tok(0), tok(1)] + [full(a) for a in ops],
        out_specs=[tok(0), tok(1)],
        out_shape=[y_shape, y_shape],
        scratch_shapes=[pltpu.VMEM((slabs, SUB * pairs_per_slab, LANES), F32)] * 4,
        compiler_params=_params(("arbitrary", "arbitrary")),
        name="s5_scan",
    )(u5, u5, *ops)


def _outproj_body(*refs, col_major, last, n_batch, n_lat_tiles):
    n_tok = 1 if last else 2
    lat_ref = refs[0]
    ctx_ref = None if last else refs[1]
    (mod_ref, yf_ref, yb_ref, z_ref, of_ref, ob_ref, gg_ref, sf_ref, sb_ref, u5_ref, sg_ref,
     snw_ref, gnw_ref, s5d_ref, gluw_ref, glub_ref, wout_ref, fnw_ref, o_ref) = refs[n_tok:]
    b, j = pl.program_id(0), pl.program_id(1)
    is_ctx = j == n_lat_tiles
    res = _load_tile(lat_ref, ctx_ref, col_major, is_ctx, jnp.minimum(j, n_lat_tiles - 1))
    gate = mod_ref[pl.ds(jnp.where(is_ctx, n_batch, b), 1), 2 * D_MODEL:3 * D_MODEL]

    def rms(v):
        return v * lax.rsqrt(jnp.mean(v * v, axis=-1, keepdims=True) + EPS)

    y = (yf_ref[0].astype(F32) + yb_ref[0].astype(F32)) * _silu(z_ref[0].astype(F32))
    half = SSD_WIDTH // 2
    ssd = jnp.concatenate([rms(y[:, :half]), rms(y[:, half:])], axis=1) * snw_ref[...]
    out = _dot(ssd.astype(BF16), wout_ref[0:SSD_WIDTH, :])

    o = of_ref[0].astype(F32) + ob_ref[0].astype(F32)
    gla = jnp.concatenate([rms(o[:, h * GLA_DV:(h + 1) * GLA_DV]) for h in range(GLA_HEADS)], axis=1)
    gla = gla * gnw_ref[...] * _silu(gg_ref[0].astype(F32))
    out = out + _dot(gla.astype(BF16), wout_ref[SSD_WIDTH:SSD_WIDTH + GLA_WIDTH, :])

    slabs = range(S5_WIDTH // LANES)
    y5 = jnp.concatenate([sf_ref[0, k] + sb_ref[0, k] for k in slabs], axis=1)
    u5 = jnp.concatenate([u5_ref[0, k] for k in slabs], axis=1)
    y5 = jax.nn.gelu(y5 + s5d_ref[...] * u5)
    pr = _dot(y5.astype(BF16), gluw_ref[...]) + glub_ref[...]
    s5 = pr[:, :S5_WIDTH] * jax.nn.sigmoid(pr[:, S5_WIDTH:]) * _silu(sg_ref[0].astype(F32))
    out = out + _dot(s5.astype(BF16), wout_ref[SSD_WIDTH + GLA_WIDTH:, :])

    h_new = res + gate * out
    if last:
        h_new = rms(h_new) * fnw_ref[...]
    if col_major:
        rows = o_ref.shape[1]
        base = _col_base(j, rows)
        for k in range(TILE // rows):
            o_ref[0, :, base + k, :] = h_new[k * rows:(k + 1) * rows, :]
    else:
        o_ref[0] = h_new


def _outproj(lat, ctx, ctx_block, mod_l, y_ssd, z, o_gla, gg, y_s5, u5, sg, ssd_norm_w, gla_norm_w, s5_d,
             glu_w, glu_b, w_out, final_norm_w, *, col_major, last, n_batch, seq, rows):
    n_lat_tiles = seq // TILE
    n_tiles = n_lat_tiles if last else n_lat_tiles + 1
    tok = lambda width: pl.BlockSpec((1, TILE, width), lambda b, j: (b, j, 0))
    both = lambda width: [tok(width), tok(width)]
    slabs = S5_WIDTH // LANES
    s5_tok = pl.BlockSpec((1, slabs, TILE, LANES), lambda b, j: (b, 0, j, 0))
    full = lambda a: pl.BlockSpec(a.shape, lambda b, j: (0,) * a.ndim)
    small = [ssd_norm_w, gla_norm_w, s5_d, glu_w, glu_b, w_out, final_norm_w]
    tokens = (lat,) if last else (lat, ctx)
    if col_major:
        per_block = COL_W // (TILE // rows)
        out_spec = pl.BlockSpec((1, rows, COL_W, D_MODEL), lambda b, j: (b, 0, j // per_block, 0))
        out_shape = jax.ShapeDtypeStruct((n_batch, rows, seq // rows, D_MODEL), F32)
    else:
        out_spec = tok(D_MODEL)
        out_shape = jax.ShapeDtypeStruct((n_batch, n_tiles * TILE, D_MODEL), F32)
    return pl.pallas_call(
        functools.partial(_outproj_body, col_major=col_major, last=last, n_batch=n_batch,
                          n_lat_tiles=n_lat_tiles),
        grid=(n_batch, n_tiles),
        in_specs=_token_specs(n_lat_tiles, rows, col_major, not last, ctx_block)
        + [full(mod_l)] + both(SSD_WIDTH) + [tok(SSD_WIDTH)] + both(GLA_WIDTH) + [tok(GLA_WIDTH)]
        + [s5_tok, s5_tok, s5_tok, tok(S5_WIDTH)]
        + [full(a) for a in small],
        out_specs=out_spec,
        out_shape=out_shape,
        compiler_params=_params(("arbitrary", "arbitrary")),
        name="outproj_col" if col_major else "outproj_row",
    )(*tokens, mod_l, *y_ssd, z, *o_gla, gg, *y_s5, u5, sg, *small)


def _pack_w_in(w):
    sizes = (1024, SSD_CONV_DIM, 32, 256, 256, 512, 512, 32, 512, 512)
    parts, start = [], 0
    for size in sizes:
        parts.append(w[:, start:start + size])
        start += size
    z, xbc, dt, q, k, v, gg, lr, u5, sg = parts
    pad = lambda a: jnp.pad(a, ((0, 0), (0, LANES - a.shape[1])))
    cols = [z, xbc, pad(jnp.concatenate([dt, lr], axis=1)), q, k, v, gg, u5, sg]
    return jnp.concatenate(cols, axis=1).astype(BF16)


def kernel(x, c, ctx, c_ctx, norm_w, mod_w, mod_b, w_in, w_out, ssd_conv_w, ssd_conv_b, ssd_a_log, ssd_dt_bias, ssd_d, ssd_norm_w, gla_w_lr, gla_b_lr, gla_norm_w, s5_lam_re, s5_lam_im, s5_log_step, s5_b_re, s5_b_im, s5_c_re, s5_c_im, s5_d, s5_glu_w, s5_glu_b, final_norm_w):
    n_batch, seq, d_model = x.shape
    depth = norm_w.shape[0]
    rows = seq // GRID_W
    assert d_model == D_MODEL and ctx.shape[1] == CTX_LEN == TILE
    assert seq % TILE == 0 and TILE % rows == 0
    assert depth == 2, "layer 0 row-major with context update, layer 1 column-major and last"
    n_lat_tiles = seq // TILE

    mod_rows = -(-(n_batch + 1) // 8) * 8
    cc = jnp.zeros((mod_rows, d_model), F32).at[:n_batch].set(c).at[n_batch].set(c_ctx)
    mod = _modulation(cc, mod_w, mod_b)

    lat, ctx_arr, ctx_block = x, ctx, 0
    for l in range(depth):
        col_major = l % 2 == 1
        last = l == depth - 1
        if col_major:
            lat_in = lat.reshape(n_batch, lat.shape[1] // GRID_W, GRID_W, d_model)
        else:
            lat_in = lat
        z, xc, small, qkv, gg, u5, sg = _inproj(
            lat_in, ctx_arr, ctx_block, mod[l], norm_w[l].reshape(1, -1), _pack_w_in(w_in[l]),
            ssd_conv_w[l], ssd_conv_b[l].reshape(1, -1),
            col_major=col_major, n_batch=n_batch, seq=seq, rows=rows)
        pad_h = lambda a: jnp.pad(a.reshape(1, -1), ((0, 0), (SMALL_DT, LANES - SMALL_DT - a.size)))
        y_ssd = _ssd_scan(xc, small, pad_h(ssd_dt_bias[l]), pad_h(ssd_a_log[l]),
                          jnp.repeat(ssd_d[l], SSD_HEADDIM).reshape(1, -1), seq=seq)

        w_lr = jnp.stack([jnp.pad(gla_w_lr[l, d], ((SMALL_LR + GLA_RANK * d, LANES - SMALL_LR - GLA_RANK * (d + 1)), (0, 0)))
                          for d in (0, 1)])
        o_gla = _gla_scan(qkv, small, w_lr, gla_b_lr[l].reshape(2, 1, -1), seq=seq)

        ops = _s5_prep(s5_lam_re[l], s5_lam_im[l], s5_log_step[l], s5_b_re[l], s5_b_im[l],
                       s5_c_re[l], s5_c_im[l])
        ops = _s5_pair_ops(*ops)
        y_s5 = _s5_scan(u5, ops, seq=seq)

        out = _outproj(
            lat_in, ctx_arr, ctx_block, mod[l], y_ssd, z, o_gla, gg, y_s5, u5, sg,
            ssd_norm_w[l].reshape(1, -1), jnp.tile(gla_norm_w[l], GLA_HEADS).reshape(1, -1),
            s5_d[l].reshape(1, -1), s5_glu_w[l].astype(BF16), s5_glu_b[l].reshape(1, -1),
            w_out[l].astype(BF16), final_norm_w.reshape(1, -1),
            col_major=col_major, last=last, n_batch=n_batch, seq=seq, rows=rows)
        if last:
            return out.reshape(n_batch, seq, d_model)
        lat, ctx_arr, ctx_block = out, out, n_lat_tiles
```

```python
import functools

import jax
import jax.numpy as jnp
from jax import lax
from jax.experimental import pallas as pl
from jax.experimental.pallas import tpu as pltpu

F32, BF16 = jnp.float32, jnp.bfloat16
STORE = BF16
HIGHEST = lax.Precision.HIGHEST
EPS = 1e-6

D_MODEL = 1024
GRID_W = 64
CTX_LEN = 256
TILE = 256
SCAN_BATCH = 2
COL_W = 8
LANES = 128
SUB = 8
CONV_COLS = 256
VMEM_LIMIT = 56 * 1024 * 1024

SSD_WIDTH, SSD_HEADS, SSD_HEADDIM, SSD_STATE, SSD_CHUNK = 1024, 16, 64, 128, 128
SSD_CONV_DIM, SSD_CONV_K = 1536, 5
GLA_HEADS, GLA_DK, GLA_DV, GLA_RANK, GLA_TAU, GLA_CHUNK = 4, 64, 128, 16, 16.0, 64
GLA_WIDTH = GLA_HEADS * GLA_DV
S5_WIDTH, S5_GROUP, S5_GROUPS, S5_STATE = 512, 16, 32, 64
S5_T = LANES // S5_GROUP
MIX_WIDTH = SSD_WIDTH + GLA_WIDTH + S5_WIDTH

C_Z, C_XBC, C_SMALL = 0, 1024, 2560
C_QKV, C_GG, C_U5, C_SG, C_END = 2688, 3712, 4224, 4736, 5248
SMALL_DT, SMALL_LR = 0, 32


def _params(semantics):
    return pltpu.CompilerParams(dimension_semantics=semantics, vmem_limit_bytes=VMEM_LIMIT)


def _silu(v):
    return v * jax.nn.sigmoid(v)


def _dot(a, b):
    return jnp.dot(a, b, preferred_element_type=F32)


def _dot_nt(a, b):
    return lax.dot_general(a, b, (((1,), (1,)), ((), ())), preferred_element_type=F32)


def _dot_exact_lhs(m_bf16, v, terms=3):
    out, rest = None, v
    for _ in range(terms):
        piece = rest.astype(BF16)
        rest = rest - piece.astype(F32)
        out = _dot(m_bf16, piece) if out is None else out + _dot(m_bf16, piece)
    return out


def _iota(shape, dim):
    return lax.broadcasted_iota(jnp.int32, shape, dim)


def _interleave(*stages):
    active = list(stages)
    while active:
        for gen in list(active):
            try:
                next(gen)
            except StopIteration:
                active.remove(gen)


def _mod_body(cc_ref, w_ref, b_ref, o_ref):
    s = _silu(cc_ref[...])
    o_ref[0] = jnp.dot(s, w_ref[0], precision=HIGHEST, preferred_element_type=F32) + b_ref[0]


def _modulation(cc, mod_w, mod_b):
    depth, d, three_d = mod_w.shape
    rows = cc.shape[0]
    return pl.pallas_call(
        _mod_body,
        grid=(depth, three_d // d),
        in_specs=[pl.BlockSpec((rows, d), lambda l, n: (0, 0)),
                  pl.BlockSpec((1, d, d), lambda l, n: (l, 0, n)),
                  pl.BlockSpec((1, 1, d), lambda l, n: (l, 0, n))],
        out_specs=pl.BlockSpec((1, rows, d), lambda l, n: (l, 0, n)),
        out_shape=jax.ShapeDtypeStruct((depth, rows, three_d), F32),
        compiler_params=_params(("arbitrary", "arbitrary")),
        name="modulation",
    )(cc, mod_w, mod_b.reshape(depth, 1, three_d))


def _col_base(j, rows):
    cols = TILE // rows
    return (j % (COL_W // cols)) * cols


def _load_tile(lat_ref, ctx_ref, col_major, is_ctx, j):
    if col_major:
        rows = lat_ref.shape[1]
        base = _col_base(j, rows)
        lat = jnp.concatenate([lat_ref[0, :, base + k, :] for k in range(TILE // rows)], axis=0)
    else:
        lat = lat_ref[0]
    if ctx_ref is None:
        return lat
    return jnp.where(is_ctx, ctx_ref[0], lat)


def _token_specs(n_lat_tiles, rows, col_major, with_ctx, ctx_block):
    last = n_lat_tiles - 1
    if col_major:
        per_block = COL_W // (TILE // rows)
        lat = pl.BlockSpec((1, rows, COL_W, D_MODEL), lambda b, j: (b, 0, jnp.minimum(j, last) // per_block, 0))
    else:
        lat = pl.BlockSpec((1, TILE, D_MODEL), lambda b, j: (b, jnp.minimum(j, last), 0))
    if not with_ctx:
        return [lat]
    return [lat, pl.BlockSpec((1, TILE, D_MODEL), lambda b, j: (b, ctx_block, 0))]


def _halo_specs(n_lat_tiles, rows, col_major, seq):
    last = n_lat_tiles - 1
    if col_major:
        cols = TILE // rows
        tile = lambda j: jnp.minimum(j, last)
        prev = pl.BlockSpec((1, SUB, COL_W, D_MODEL),
                            lambda b, j: (b, rows // SUB - 1, jnp.maximum(tile(j) * cols - 1, 0) // COL_W, 0))
        nxt = pl.BlockSpec((1, SUB, COL_W, D_MODEL),
                           lambda b, j: (b, 0, jnp.minimum((tile(j) + 1) * cols, GRID_W - 1) // COL_W, 0))
    else:
        per_tile, n_blk = TILE // SUB, seq // SUB
        tile = lambda j: jnp.minimum(j, last)
        prev = pl.BlockSpec((1, SUB, D_MODEL), lambda b, j: (b, jnp.maximum(tile(j) * per_tile - 1, 0), 0))
        nxt = pl.BlockSpec((1, SUB, D_MODEL), lambda b, j: (b, jnp.minimum((tile(j) + 1) * per_tile, n_blk - 1), 0))
    return [prev, nxt]


def _load_halo(prev_ref, next_ref, col_major, j, rows):
    if col_major:
        cols = TILE // rows
        prev = prev_ref[0, :, jnp.maximum(j * cols - 1, 0) % COL_W, :]
        nxt = next_ref[0, :, jnp.minimum((j + 1) * cols, GRID_W - 1) % COL_W, :]
    else:
        prev, nxt = prev_ref[0], next_ref[0]
    return jnp.concatenate([prev, nxt], axis=0)


def _inproj_body(*refs, col_major, n_batch, n_lat_tiles):
    lat_ref, ctx_ref, prev_ref, next_ref, mod_ref, nw_ref, w_ref, cw_ref, cb_ref = refs[:9]
    z_ref, xc_ref, small_ref, qkv_ref, gg_ref, u5_ref, sg_ref = refs[9:]
    b, j = pl.program_id(0), pl.program_id(1)
    is_ctx = j == n_lat_tiles
    jj = jnp.minimum(j, n_lat_tiles - 1)
    x = _load_tile(lat_ref, ctx_ref, col_major, is_ctx, jj)
    m = mod_ref[pl.ds(jnp.where(is_ctx, n_batch, b), 1), :]
    shift, scale = m[:, :D_MODEL], m[:, D_MODEL:2 * D_MODEL]

    def modulated(v):
        v = v * lax.rsqrt(jnp.mean(v * v, axis=-1, keepdims=True) + EPS) * nw_ref[...]
        return (v * (1.0 + scale) + shift).astype(BF16)

    u = modulated(x)

    u_halo = modulated(_load_halo(prev_ref, next_ref, col_major, jj, lat_ref.shape[1]))
    has_prev = jnp.logical_and(jnp.logical_not(is_ctx), j > 0)
    has_next = jnp.logical_and(jnp.logical_not(is_ctx), j < n_lat_tiles - 1)

    z_ref[0] = _dot(u, w_ref[:, C_Z:C_XBC]).astype(z_ref.dtype)
    for c0 in range(0, SSD_CONV_DIM, CONV_COLS):
        cols = slice(C_XBC + c0, C_XBC + c0 + CONV_COLS)
        halo = _dot(u_halo, w_ref[:, cols])
        ext = jnp.concatenate([jnp.where(has_prev, halo[:SUB], 0.0), _dot(u, w_ref[:, cols]),
                               jnp.where(has_next, halo[SUB:], 0.0)], axis=0)
        acc = jnp.broadcast_to(cb_ref[:, c0:c0 + CONV_COLS], (TILE, CONV_COLS))
        for k in range(SSD_CONV_K):
            off = SUB + k - SSD_CONV_K // 2
            acc = acc + cw_ref[k:k + 1, c0:c0 + CONV_COLS] * ext[off:off + TILE, :]
        xc_ref[0, :, c0:c0 + CONV_COLS] = _silu(acc).astype(xc_ref.dtype)
    small_ref[0] = _dot(u, w_ref[:, C_SMALL:C_QKV])
    qkv_ref[0] = _dot(u, w_ref[:, C_QKV:C_GG]).astype(qkv_ref.dtype)
    gg_ref[0] = _dot(u, w_ref[:, C_GG:C_U5]).astype(gg_ref.dtype)
    u5 = _dot(u, w_ref[:, C_U5:C_SG])
    for k in range(S5_WIDTH // LANES):
        u5_ref[0, k] = u5[:, k * LANES:(k + 1) * LANES]
    sg_ref[0] = _dot(u, w_ref[:, C_SG:C_END]).astype(sg_ref.dtype)


def _inproj(lat, ctx, ctx_block, mod_l, norm_w, w_packed, conv_w, conv_b, *, col_major, n_batch, seq, rows):
    n_lat_tiles = seq // TILE
    lt = seq + CTX_LEN
    tok = lambda width: pl.BlockSpec((1, TILE, width), lambda b, j: (b, j, 0))
    full = lambda a: pl.BlockSpec(a.shape, lambda b, j: (0,) * a.ndim)
    shape = lambda width, dtype=STORE: jax.ShapeDtypeStruct((n_batch, lt, width), dtype)
    return pl.pallas_call(
        functools.partial(_inproj_body, col_major=col_major, n_batch=n_batch, n_lat_tiles=n_lat_tiles),
        grid=(n_batch, n_lat_tiles + 1),
        in_specs=_token_specs(n_lat_tiles, rows, col_major, True, ctx_block)
        + _halo_specs(n_lat_tiles, rows, col_major, seq)
        + [full(mod_l), full(norm_w), full(w_packed), full(conv_w), full(conv_b)],
        out_specs=[tok(1024), tok(SSD_CONV_DIM), tok(LANES), tok(1024), tok(512),
                   pl.BlockSpec((1, S5_WIDTH // LANES, TILE, LANES), lambda b, j: (b, 0, j, 0)),
                   tok(512)],
        out_shape=[shape(1024), shape(SSD_CONV_DIM), shape(LANES, F32), shape(1024), shape(512),
                   jax.ShapeDtypeStruct((n_batch, S5_WIDTH // LANES, lt, LANES), F32), shape(512)],
        compiler_params=_params(("arbitrary", "arbitrary")),
        name="inproj_col" if col_major else "inproj_row",
    )(lat, ctx, lat, lat, mod_l, norm_w, w_packed, conv_w, conv_b)


def _ssd_body(xcf_ref, dtf_ref, xcb_ref, dtb_ref, bias_ref, alog_ref, dsk_ref, yf_ref, yb_ref, stf_ref, stb_ref):
    @pl.when(pl.program_id(1) == 0)
    def _():
        stf_ref[...] = jnp.zeros_like(stf_ref)
        stb_ref[...] = jnp.zeros_like(stb_ref)

    chains = []
    for bi in range(xcf_ref.shape[0]):
        chains.append(_ssd_dir(xcf_ref, dtf_ref, bias_ref, alog_ref, dsk_ref, yf_ref, stf_ref, bi, rev=False))
        chains.append(_ssd_dir(xcb_ref, dtb_ref, bias_ref, alog_ref, dsk_ref, yb_ref, stb_ref, bi, rev=True))
    _interleave(*chains)


def _ssd_dir(xc_ref, dt_ref, bias_ref, alog_ref, dsk_ref, y_ref, st_ref, bi, *, rev):
    t = SSD_CHUNK
    d = 1 if rev else 0
    off = SMALL_DT + SSD_HEADS * d
    dt = jax.nn.softplus(dt_ref[bi] + bias_ref[...])
    da = dt * -jnp.exp(alog_ref[...])
    row, col = _iota((t, t), 0), _iota((t, t), 1)
    tri = row <= col if rev else row >= col
    p = _dot_exact_lhs(jnp.where(tri, 1.0, 0.0).astype(BF16), da)
    total = jnp.sum(da, axis=0, keepdims=True)
    e_tot = jnp.exp(total)
    p_t = p.T
    dt_t = dt.T
    w_t = (dt * jnp.exp(total - p)).T
    lo = _iota((1, LANES), 1) < SSD_HEADDIM
    hi = jnp.logical_not(lo)
    pairs_per_group = SSD_HEADS // 2 // 2
    yield
    for g in range(2):
        bg = xc_ref[bi, :, SSD_WIDTH + g * SSD_STATE:SSD_WIDTH + (g + 1) * SSD_STATE]
        cg = xc_ref[bi, :, SSD_WIDTH + (2 + g) * SSD_STATE:SSD_WIDTH + (3 + g) * SSD_STATE]
        cg_b = cg.astype(BF16)
        cb = _dot_nt(cg_b, bg.astype(BF16))
        bg_t = bg.astype(F32).T
        for q in range(pairs_per_group):
            hp = g * pairs_per_group + q
            h0, h1 = off + 2 * hp, off + 2 * hp + 1
            pb0 = jnp.broadcast_to(p[:, h0:h0 + 1], (t, LANES))
            pb1 = jnp.broadcast_to(p[:, h1:h1 + 1], (t, LANES))
            xp = xc_ref[bi, :, hp * LANES:(hp + 1) * LANES]
            st = st_ref[bi, hp]
            acc = jnp.exp(jnp.where(lo, pb0, pb1)) * _dot(cg_b, st.astype(BF16))
            if not rev:
                acc = acc + dsk_ref[:, hp * LANES:(hp + 1) * LANES] * xp.astype(F32)
            st = jnp.where(lo, e_tot[:, h0:h0 + 1], e_tot[:, h1:h1 + 1]) * st
            for h, pb, sel in ((h0, pb0, lo), (h1, pb1, hi)):
                x_h = jnp.where(sel, xp, jnp.zeros_like(xp)).astype(BF16)
                decay = jnp.where(tri, jnp.exp(pb - p_t[h:h + 1, :]), 0.0)
                acc = acc + _dot((cb * decay * dt_t[h:h + 1, :]).astype(BF16), x_h)
                st = st + _dot((bg_t * w_t[h:h + 1, :]).astype(BF16), x_h)
            y_ref[bi, :, hp * LANES:(hp + 1) * LANES] = acc.astype(y_ref.dtype)
            st_ref[bi, hp] = st
            yield


def _ssd_scan(xc, dt, dt_bias, a_log, dsk, *, seq):
    n_batch, total, _ = xc.shape
    nc = total // SSD_CHUNK
    lat_c = seq // SSD_CHUNK
    ctx_c = nc - lat_c
    fwd = lambda i: jnp.where(i < ctx_c, lat_c + i, i - ctx_c)
    bwd = lambda i: nc - 1 - i
    full = lambda a: pl.BlockSpec(a.shape, lambda b, i: (0,) * a.ndim)
    nb = SCAN_BATCH if n_batch % SCAN_BATCH == 0 else 1
    xc_spec = lambda chunk: pl.BlockSpec((nb, SSD_CHUNK, SSD_CONV_DIM), lambda b, i: (b, chunk(i), 0))
    dt_spec = lambda chunk: pl.BlockSpec((nb, SSD_CHUNK, LANES), lambda b, i: (b, chunk(i), 0))
    y_spec = lambda chunk: pl.BlockSpec((nb, SSD_CHUNK, SSD_WIDTH), lambda b, i: (b, chunk(i), 0))
    y_shape = jax.ShapeDtypeStruct((n_batch, total, SSD_WIDTH), STORE)
    return pl.pallas_call(
        _ssd_body,
        grid=(n_batch // nb, nc),
        in_specs=[xc_spec(fwd), dt_spec(fwd), xc_spec(bwd), dt_spec(bwd),
                  full(dt_bias), full(a_log), full(dsk)],
        out_specs=[y_spec(fwd), y_spec(bwd)],
        out_shape=[y_shape, y_shape],
        scratch_shapes=[pltpu.VMEM((nb, SSD_HEADS // 2, SSD_STATE, LANES), F32)] * 2,
        compiler_params=_params(("arbitrary", "arbitrary")),
        name="ssd_scan",
    )(xc, dt, xc, dt, dt_bias, a_log, dsk)


def _tile_index(d, i, n_lat_tiles):
    return jnp.where(i == 0, n_lat_tiles, jnp.where(d == 0, i - 1, n_lat_tiles - i))


def _gla_body(qkvf_ref, lrf_ref, qkvb_ref, lrb_ref, wlr_ref, blr_ref, of_ref, ob_ref, sf_ref, sb_ref):
    @pl.when(pl.program_id(1) == 0)
    def _():
        sf_ref[...] = jnp.zeros_like(sf_ref)
        sb_ref[...] = jnp.zeros_like(sb_ref)

    chains = []
    for bi in range(qkvf_ref.shape[0]):
        chains.append(_gla_dir(qkvf_ref, lrf_ref, wlr_ref, blr_ref, of_ref, sf_ref, bi, rev=False))
        chains.append(_gla_dir(qkvb_ref, lrb_ref, wlr_ref, blr_ref, ob_ref, sb_ref, bi, rev=True))
    _interleave(*chains)


def _gla_dir(qkv_ref, lr_ref, wlr_ref, blr_ref, o_ref, s_ref, bi, *, rev):
    d = 1 if rev else 0
    t = GLA_CHUNK
    n_chunks = TILE // t
    kw = GLA_HEADS * GLA_DK
    row, col = _iota((TILE, TILE), 0), _iota((TILE, TILE), 1)
    same = row // t == col // t
    tri = jnp.where(same, row - col, 1 if rev else -1)
    tri = tri <= 0 if rev else tri >= 0
    lo = _iota((1, LANES), 1) < GLA_DK
    hi = jnp.logical_not(lo)
    eye = _iota((LANES, LANES), 0) == _iota((LANES, LANES), 1)
    row_lo = _iota((LANES, 1), 0) < GLA_DK

    logit = jnp.dot(lr_ref[bi], wlr_ref[d], precision=HIGHEST, preferred_element_type=F32) + blr_ref[d]
    yield
    g = jax.nn.log_sigmoid(logit) / GLA_TAU
    p = _dot_exact_lhs(jnp.where(tri, 1.0, 0.0).astype(BF16), g, terms=2)
    total = jnp.concatenate(
        [jnp.broadcast_to(p[(c * t if rev else (c + 1) * t - 1):(c * t + 1 if rev else (c + 1) * t), :], (t, kw))
         for c in range(n_chunks)], axis=0)
    yield
    q = qkv_ref[bi, :, 0:kw].astype(F32) * GLA_DK ** -0.5
    k = qkv_ref[bi, :, kw:2 * kw].astype(F32)
    qe = q * jnp.exp(p)
    ke = k * jnp.exp(-p)
    kd = k * jnp.exp(total - p)
    e_tot = jnp.exp(total)
    order = range(n_chunks - 1, -1, -1) if rev else range(n_chunks)
    pairs = range(GLA_HEADS // 2)
    lanes = lambda pr: slice(pr * LANES, (pr + 1) * LANES)
    qms, intra, vs = {}, {}, {}
    for pr in pairs:
        ke_p = ke[:, lanes(pr)].astype(BF16)
        for e, sel in ((0, lo), (1, hi)):
            h = 2 * pr + e
            qms[h] = jnp.where(sel, qe[:, lanes(pr)], 0.0).astype(BF16)
            vs[h] = qkv_ref[bi, :, 2 * kw + h * GLA_DV:2 * kw + (h + 1) * GLA_DV].astype(BF16)
            attn = jnp.where(tri, _dot_nt(qms[h], ke_p), 0.0).astype(BF16)
            yield
            intra[h] = _dot(attn, vs[h])
            yield
    upd, dec = {}, {}
    for pr in pairs:
        for c in order:
            rows = slice(c * t, (c + 1) * t)
            kd_t = kd[rows, lanes(pr)].T.astype(BF16)
            upd[pr, c] = jnp.where(row_lo, _dot(kd_t, vs[2 * pr][rows]), _dot(kd_t, vs[2 * pr + 1][rows]))
            dec[pr, c] = jnp.sum(jnp.where(eye, e_tot[c * t:c * t + 1, lanes(pr)], 0.0), axis=1, keepdims=True)
        yield
    for pr in pairs:
        s = s_ref[bi, pr]
        for c in order:
            rows = slice(c * t, (c + 1) * t)
            s_b = s.astype(BF16)
            for h in (2 * pr, 2 * pr + 1):
                o_ref[bi, rows, h * GLA_DV:(h + 1) * GLA_DV] = (intra[h][rows] + _dot(qms[h][rows], s_b)).astype(o_ref.dtype)
            s = dec[pr, c] * s + upd[pr, c]
            yield
        s_ref[bi, pr] = s


def _gla_scan(qkv, lr, w_lr, b_lr, *, seq):
    n_batch, total, _ = qkv.shape
    n_lat_tiles = seq // TILE
    tile = lambda d: (lambda i: _tile_index(d, i, n_lat_tiles))
    full = lambda a: pl.BlockSpec(a.shape, lambda b, i: (0,) * a.ndim)
    nb = SCAN_BATCH if n_batch % SCAN_BATCH == 0 else 1
    qkv_spec = lambda d: pl.BlockSpec((nb, TILE, 1024), lambda b, i: (b, tile(d)(i), 0))
    lr_spec = lambda d: pl.BlockSpec((nb, TILE, LANES), lambda b, i: (b, tile(d)(i), 0))
    o_spec = lambda d: pl.BlockSpec((nb, TILE, GLA_WIDTH), lambda b, i: (b, tile(d)(i), 0))
    o_shape = jax.ShapeDtypeStruct((n_batch, total, GLA_WIDTH), STORE)
    return pl.pallas_call(
        _gla_body,
        grid=(n_batch // nb, n_lat_tiles + 1),
        in_specs=[qkv_spec(0), lr_spec(0), qkv_spec(1), lr_spec(1), full(w_lr), full(b_lr)],
        out_specs=[o_spec(0), o_spec(1)],
        out_shape=[o_shape, o_shape],
        scratch_shapes=[pltpu.VMEM((nb, GLA_HEADS // 2, LANES, GLA_DV), F32)] * 2,
        compiler_params=_params(("arbitrary", "arbitrary")),
        name="gla_scan",
    )(qkv, lr, qkv, lr, w_lr, b_lr)


S5_ROWS = TILE // S5_T
SLABS_IN_FLIGHT = 2


def _s5_prep_body(ls_ref, lrr_ref, lir_ref, lrc_ref, lic_ref, p1_ref, p2_ref, q1_ref, q2_ref,
                  g_ref, win_ref, wout_ref, ar_ref, ai_ref):
    fwd = pl.program_id(0) % 2 == 0
    n = LANES
    step = jnp.exp(ls_ref[0, 0])
    lo = _iota((1, n), 1) < S5_STATE
    top = _iota((n, 1), 0) < S5_STATE

    lre, lim = lrr_ref[0, 0], lir_ref[0, 0]
    lr, ang = lre * step, lim * step
    er = jnp.exp(lr)
    nr, ni = er * jnp.cos(ang) - 1.0, er * jnp.sin(ang)
    den = lre * lre + lim * lim
    cr, ci = (nr * lre + ni * lim) / den, (ni * lre - nr * lim) / den
    p1, p2 = p1_ref[0, 0], p2_ref[0, 0]
    bbr, bbi = cr * p1 - ci * p2, cr * p2 + ci * p1
    s_idx = _iota((n, 1), 0) // S5_GROUP
    pos_s = jnp.where(fwd, s_idx, S5_T - 1 - s_idx)
    xs = (S5_T - 1 - pos_s).astype(F32)
    e = jnp.exp(xs * lr)
    l_re, l_im = e * jnp.cos(xs * ang), e * jnp.sin(xs * ang)
    w_in = jnp.where(lo, bbr * l_re - bbi * l_im, bbr * l_im + bbi * l_re)
    win_ref[0, 0] = w_in.astype(BF16)

    lrc, angc = lrc_ref[0, 0] * step, lic_ref[0, 0] * step
    q1, q2 = q1_ref[0, 0], q2_ref[0, 0]
    t_idx = _iota((1, n), 1) // S5_GROUP
    pos_t = jnp.where(fwd, t_idx, S5_T - 1 - t_idx)

    def out_op(xt):
        ex = jnp.exp(lrc * xt)
        o_re, o_im = ex * jnp.cos(angc * xt), ex * jnp.sin(angc * xt)
        return jnp.where(top, q1 * o_re - q2 * o_im, -(q1 * o_im + q2 * o_re))

    wout_ref[0, 0] = out_op((pos_t + 1).astype(F32)).astype(BF16)
    g = jnp.dot(w_in, out_op((pos_t - (S5_T - 1)).astype(F32)), precision=HIGHEST,
                preferred_element_type=F32)
    g_ref[0, 0] = jnp.where(pos_t >= pos_s, g, 0.0).astype(BF16)

    r_idx = _iota((S5_ROWS, 1), 0)
    xe = (S5_T * (jnp.where(fwd, r_idx, S5_ROWS - 1 - r_idx) + 1)).astype(F32)
    ee = jnp.exp(xe * lr)
    a_im = ee * jnp.sin(xe * ang)
    ar_ref[0, 0] = ee * jnp.cos(xe * ang)
    ai_ref[0, 0] = jnp.where(lo, -a_im, a_im)


def _s5_prep(lam_re, lam_im, log_step, b_re, b_im, c_re, c_im):
    n = LANES
    ld = 2 * lam_re.shape[0]
    dup_l = lambda a: jnp.concatenate([a, a], axis=-1)
    dup_r = lambda a: jnp.concatenate([a, a], axis=-2)
    merge = lambda a: a.reshape((ld,) + a.shape[2:])
    lrr, lir = dup_l(merge(lam_re))[:, :, None, :], dup_l(merge(lam_im))[:, :, None, :]
    lrc, lic = lrr.reshape(ld, S5_GROUPS, n, 1), lir.reshape(ld, S5_GROUPS, n, 1)
    ls = log_step.reshape(ld, S5_GROUPS, 1, 1)
    tile_rows = lambda a: dup_l(jnp.tile(jnp.swapaxes(a, -1, -2), (1, 1, S5_T, 1)))
    p1, p2 = tile_rows(b_re), tile_rows(b_im)
    tile_cols = lambda a: dup_r(jnp.tile(jnp.swapaxes(merge(a), -1, -2), (1, 1, 1, S5_T)))
    q1, q2 = tile_cols(c_re), tile_cols(c_im)
    dg = lambda *blk: pl.BlockSpec((1, 1) + blk, lambda d, g: (d, g, 0, 0))
    gg = lambda *blk: pl.BlockSpec((1, 1) + blk, lambda d, g: (d // 2, g, 0, 0))
    out = lambda rows, dtype: jax.ShapeDtypeStruct((ld, S5_GROUPS, rows, n), dtype)
    return pl.pallas_call(
        _s5_prep_body,
        grid=(ld, S5_GROUPS),
        in_specs=[dg(1, 1), dg(1, n), dg(1, n), dg(n, 1), dg(n, 1), gg(n, n), gg(n, n), dg(n, n), dg(n, n)],
        out_specs=[dg(n, n), dg(n, n), dg(n, n), dg(S5_ROWS, n), dg(S5_ROWS, n)],
        out_shape=[out(n, BF16), out(n, BF16), out(n, BF16), out(S5_ROWS, F32), out(S5_ROWS, F32)],
        compiler_params=_params(("arbitrary", "arbitrary")),
        name="s5_prep",
    )(ls, lrr, lir, lrc, lic, p1, p2, q1, q2)


def _s5_pair_ops(g, win, wout, ar, ai):
    hs = S5_STATE
    cat = jnp.concatenate
    pair = lambda a: (a[:, 0::2], a[:, 1::2])
    g0, g1 = pair(g)
    i0, i1 = pair(win)
    o0, o1 = pair(wout)
    z = jnp.zeros_like(g0)
    zh, zr = z[..., :hs], z[..., :hs, :]
    wa = cat([cat([g0, z, i0[..., :hs], zh, i0[..., hs:], zh], -1),
              cat([z, g1, zh, i1[..., :hs], zh, i1[..., hs:]], -1)], -2)
    wo = cat([cat([o0[..., :hs, :], zr], -1), cat([zr, o1[..., :hs, :]], -1),
              cat([o0[..., hs:, :], zr], -1), cat([zr, o1[..., hs:, :]], -1)], -2)
    a0, a1 = pair(ar)
    b0, b1 = pair(ai)
    return wa, wo, cat([a0[..., :hs], a1[..., :hs]], -1), cat([b0[..., hs:], b1[..., hs:]], -1)


def _block_transpose(arrs):
    n = len(arrs)
    block = _iota((1, LANES), 1) // S5_GROUP
    arrs = list(arrs)
    s = n // 2
    while s >= 1:
        upper = (block // s) % 2 == 1
        nxt = list(arrs)
        for i in range(n):
            if (i // s) % 2 == 0:
                a, b = arrs[i], arrs[i + s]
                nxt[i] = jnp.where(upper, pltpu.roll(b, s * S5_GROUP, 1), a)
                nxt[i + s] = jnp.where(upper, b, pltpu.roll(a, LANES - s * S5_GROUP, 1))
        arrs = nxt
        s //= 2
    return arrs


def _s5_body(uf_ref, ub_ref, wa_ref, wo_ref, are_ref, aim_ref, yf_ref, yb_ref, *h_refs):
    @pl.when(pl.program_id(1) == 0)
    def _():
        for h_ref in h_refs:
            h_ref[...] = jnp.zeros_like(h_ref)

    ops = (wa_ref, wo_ref, are_ref, aim_ref)
    slabs = []
    for bi in range(uf_ref.shape[0]):
        slabs.append(_s5_dir(uf_ref, *ops, yf_ref, *h_refs[:2], bi, rev=False))
        slabs.append(_s5_dir(ub_ref, *ops, yb_ref, *h_refs[2:], bi, rev=True))
    for k0 in range(0, S5_WIDTH // LANES, SLABS_IN_FLIGHT):
        _interleave(*[slab(k) for k in range(k0, k0 + SLABS_IN_FLIGHT) for slab in slabs])


def _s5_dir(u_ref, wa_ref, wo_ref, are_ref, aim_ref, y_ref, hre_ref, him_ref, bi, *, rev):
    d = 1 if rev else 0
    rws = S5_ROWS
    row = _iota((rws, 1), 0)
    pos = rws - 1 - row if rev else row
    per_slab = LANES // S5_GROUP
    pairs = per_slab // 2

    def earlier(v, k):
        return pltpu.roll(v, rws - k if rev else k, 0)

    def slab(k):
        parts = [u_ref[bi, k, pl.ds(t, rws, stride=S5_T), :] for t in range(S5_T)]
        h_all_re, h_all_im = hre_ref[bi, k], him_ref[bi, k]
        xs = _block_transpose(parts)
        yield
        first = []
        for q in range(pairs):
            x2 = jnp.concatenate([xs[2 * q], xs[2 * q + 1]], axis=1).astype(BF16)
            first.append(_dot(x2, wa_ref[d, k * pairs + q]))
            yield
        ys, new_re, new_im = [], [], []
        for q in range(pairs):
            pr = k * pairs + q
            res = first[q]
            y_intra, h_re, h_im = res[:, :2 * LANES], res[:, 2 * LANES:3 * LANES], res[:, 3 * LANES:]
            span = 1
            while span < SUB:
                idx = rws - span if rev else span - 1
                a_re, a_im = are_ref[d, pr, idx:idx + 1, :], aim_ref[d, pr, idx:idx + 1, :]
                s_re = jnp.where(pos % SUB >= span, earlier(h_re, span), 0.0)
                s_im = jnp.where(pos % SUB >= span, earlier(h_im, span), 0.0)
                h_re, h_im = h_re + (a_re * s_re - a_im * s_im), h_im + (a_re * s_im + a_im * s_re)
                span *= 2
            c_re, c_im = h_all_re[SUB * q:SUB * q + 1, :], h_all_im[SUB * q:SUB * q + 1, :]
            in_re, in_im = jnp.broadcast_to(c_re, (rws, LANES)), jnp.broadcast_to(c_im, (rws, LANES))
            first_block = slice(rws - SUB, rws) if rev else slice(0, SUB)
            t_re, t_im = are_ref[d, pr, first_block, :], aim_ref[d, pr, first_block, :]
            blocks_re, blocks_im = {}, {}
            for v in (range(rws // SUB - 1, -1, -1) if rev else range(rws // SUB)):
                blk = slice(SUB * v, SUB * (v + 1))
                b_re = h_re[blk] + (t_re * c_re - t_im * c_im)
                b_im = h_im[blk] + (t_re * c_im + t_im * c_re)
                blocks_re[v], blocks_im[v] = b_re, b_im
                edge = 0 if rev else SUB - 1
                c_re, c_im = b_re[edge:edge + 1, :], b_im[edge:edge + 1, :]
            h_re = jnp.concatenate([blocks_re[v] for v in range(rws // SUB)], axis=0)
            h_im = jnp.concatenate([blocks_im[v] for v in range(rws // SUB)], axis=0)
            p_re = jnp.where(pos == 0, in_re, earlier(h_re, 1))
            p_im = jnp.where(pos == 0, in_im, earlier(h_im, 1))
            new_re.append(jnp.broadcast_to(c_re, (SUB, LANES)))
            new_im.append(jnp.broadcast_to(c_im, (SUB, LANES)))
            y2 = y_intra + _dot(jnp.concatenate([p_re, p_im], axis=1).astype(BF16), wo_ref[d, pr])
            ys += [y2[:, :LANES], y2[:, LANES:]]
            yield
        hre_ref[bi, k] = jnp.concatenate(new_re, axis=0)
        him_ref[bi, k] = jnp.concatenate(new_im, axis=0)
        for t, out in enumerate(_block_transpose(ys)):
            y_ref[bi, k, pl.ds(t, rws, stride=S5_T), :] = out

    return slab


def _s5_scan(u5, ops, *, seq, layer):
    n_batch, slabs, total, _ = u5.shape
    n_lat_tiles = seq // TILE
    nb = SCAN_BATCH if n_batch % SCAN_BATCH == 0 else 1
    tok = lambda d: pl.BlockSpec((nb, slabs, TILE, LANES), lambda b, i: (b, 0, _tile_index(d, i, n_lat_tiles), 0))
    full = lambda a: pl.BlockSpec((2,) + a.shape[1:], lambda b, i: (layer, 0, 0, 0))
    pairs_per_slab = LANES // S5_GROUP // 2
    y_shape = jax.ShapeDtypeStruct((n_batch, slabs, total, LANES), F32)
    return pl.pallas_call(
        _s5_body,
        grid=(n_batch // nb, n_lat_tiles + 1),
        in_specs=[tok(0), tok(1)] + [full(a) for a in ops],
        out_specs=[tok(0), tok(1)],
        out_shape=[y_shape, y_shape],
        scratch_shapes=[pltpu.VMEM((nb, slabs, SUB * pairs_per_slab, LANES), F32)] * 4,
        compiler_params=_params(("arbitrary", "arbitrary")),
        name="s5_scan",
    )(u5, u5, *ops)


def _outproj_body(*refs, col_major, last, n_batch, n_lat_tiles):
    n_tok = 1 if last else 2
    lat_ref = refs[0]
    ctx_ref = None if last else refs[1]
    (mod_ref, yf_ref, yb_ref, z_ref, of_ref, ob_ref, gg_ref, sf_ref, sb_ref, u5_ref, sg_ref,
     snw_ref, gnw_ref, s5d_ref, gluw_ref, glub_ref, wout_ref, fnw_ref, o_ref) = refs[n_tok:]
    b, j = pl.program_id(0), pl.program_id(1)
    is_ctx = j == n_lat_tiles
    res = _load_tile(lat_ref, ctx_ref, col_major, is_ctx, jnp.minimum(j, n_lat_tiles - 1))
    gate = mod_ref[pl.ds(jnp.where(is_ctx, n_batch, b), 1), 2 * D_MODEL:3 * D_MODEL]

    def rms(v):
        return v * lax.rsqrt(jnp.mean(v * v, axis=-1, keepdims=True) + EPS)

    y = (yf_ref[0].astype(F32) + yb_ref[0].astype(F32)) * _silu(z_ref[0].astype(F32))
    half = SSD_WIDTH // 2
    ssd = jnp.concatenate([rms(y[:, :half]), rms(y[:, half:])], axis=1) * snw_ref[...]
    out = _dot(ssd.astype(BF16), wout_ref[0:SSD_WIDTH, :])

    o = of_ref[0].astype(F32) + ob_ref[0].astype(F32)
    gla = jnp.concatenate([rms(o[:, h * GLA_DV:(h + 1) * GLA_DV]) for h in range(GLA_HEADS)], axis=1)
    gla = gla * gnw_ref[...] * _silu(gg_ref[0].astype(F32))
    out = out + _dot(gla.astype(BF16), wout_ref[SSD_WIDTH:SSD_WIDTH + GLA_WIDTH, :])

    slabs = range(S5_WIDTH // LANES)
    y5 = jnp.concatenate([sf_ref[0, k] + sb_ref[0, k] for k in slabs], axis=1)
    u5 = jnp.concatenate([u5_ref[0, k] for k in slabs], axis=1)
    y5 = jax.nn.gelu(y5 + s5d_ref[...] * u5)
    pr = _dot(y5.astype(BF16), gluw_ref[...]) + glub_ref[...]
    s5 = pr[:, :S5_WIDTH] * jax.nn.sigmoid(pr[:, S5_WIDTH:]) * _silu(sg_ref[0].astype(F32))
    out = out + _dot(s5.astype(BF16), wout_ref[SSD_WIDTH + GLA_WIDTH:, :])

    h_new = res + gate * out
    if last:
        h_new = rms(h_new) * fnw_ref[...]
    if col_major:
        rows = o_ref.shape[1]
        base = _col_base(j, rows)
        for k in range(TILE // rows):
            o_ref[0, :, base + k, :] = h_new[k * rows:(k + 1) * rows, :]
    else:
        o_ref[0] = h_new


def _outproj(lat, ctx, ctx_block, mod_l, y_ssd, z, o_gla, gg, y_s5, u5, sg, ssd_norm_w, gla_norm_w, s5_d,
             glu_w, glu_b, w_out, final_norm_w, *, col_major, last, n_batch, seq, rows):
    n_lat_tiles = seq // TILE
    n_tiles = n_lat_tiles if last else n_lat_tiles + 1
    tok = lambda width: pl.BlockSpec((1, TILE, width), lambda b, j: (b, j, 0))
    both = lambda width: [tok(width), tok(width)]
    slabs = S5_WIDTH // LANES
    s5_tok = pl.BlockSpec((1, slabs, TILE, LANES), lambda b, j: (b, 0, j, 0))
    full = lambda a: pl.BlockSpec(a.shape, lambda b, j: (0,) * a.ndim)
    small = [ssd_norm_w, gla_norm_w, s5_d, glu_w, glu_b, w_out, final_norm_w]
    tokens = (lat,) if last else (lat, ctx)
    if col_major:
        per_block = COL_W // (TILE // rows)
        out_spec = pl.BlockSpec((1, rows, COL_W, D_MODEL), lambda b, j: (b, 0, j // per_block, 0))
        out_shape = jax.ShapeDtypeStruct((n_batch, rows, seq // rows, D_MODEL), F32)
    else:
        out_spec = tok(D_MODEL)
        out_shape = jax.ShapeDtypeStruct((n_batch, n_tiles * TILE, D_MODEL), F32)
    return pl.pallas_call(
        functools.partial(_outproj_body, col_major=col_major, last=last, n_batch=n_batch,
                          n_lat_tiles=n_lat_tiles),
        grid=(n_batch, n_tiles),
        in_specs=_token_specs(n_lat_tiles, rows, col_major, not last, ctx_block)
        + [full(mod_l)] + both(SSD_WIDTH) + [tok(SSD_WIDTH)] + both(GLA_WIDTH) + [tok(GLA_WIDTH)]
        + [s5_tok, s5_tok, s5_tok, tok(S5_WIDTH)]
        + [full(a) for a in small],
        out_specs=out_spec,
        out_shape=out_shape,
        compiler_params=_params(("arbitrary", "arbitrary")),
        name="outproj_col" if col_major else "outproj_row",
    )(*tokens, mod_l, *y_ssd, z, *o_gla, gg, *y_s5, u5, sg, *small)


def _pack_w_in(w):
    sizes = (1024, SSD_CONV_DIM, 32, 256, 256, 512, 512, 32, 512, 512)
    parts, start = [], 0
    for size in sizes:
        parts.append(w[..., start:start + size])
        start += size
    z, xbc, dt, q, k, v, gg, lr, u5, sg = parts
    small = jnp.concatenate([dt, lr], axis=-1)
    small = jnp.pad(small, ((0, 0),) * (w.ndim - 1) + ((0, LANES - small.shape[-1]),))
    return jnp.concatenate([z, xbc, small, q, k, v, gg, u5, sg], axis=-1).astype(BF16)


def kernel(x, c, ctx, c_ctx, norm_w, mod_w, mod_b, w_in, w_out, ssd_conv_w, ssd_conv_b, ssd_a_log, ssd_dt_bias, ssd_d, ssd_norm_w, gla_w_lr, gla_b_lr, gla_norm_w, s5_lam_re, s5_lam_im, s5_log_step, s5_b_re, s5_b_im, s5_c_re, s5_c_im, s5_d, s5_glu_w, s5_glu_b, final_norm_w):
    n_batch, seq, d_model = x.shape
    depth = norm_w.shape[0]
    rows = seq // GRID_W
    assert d_model == D_MODEL and ctx.shape[1] == CTX_LEN == TILE
    assert seq % TILE == 0 and TILE % rows == 0
    assert depth == 2, "layer 0 row-major with context update, layer 1 column-major and last"
    n_lat_tiles = seq // TILE

    mod_rows = -(-(n_batch + 1) // 8) * 8
    cc = jnp.zeros((mod_rows, d_model), F32).at[:n_batch].set(c).at[n_batch].set(c_ctx)
    mod = _modulation(cc, mod_w, mod_b)

    w_in_packed, w_out_b, glu_w_b = _pack_w_in(w_in), w_out.astype(BF16), s5_glu_w.astype(BF16)
    s5_ops = _s5_pair_ops(*_s5_prep(s5_lam_re, s5_lam_im, s5_log_step, s5_b_re, s5_b_im, s5_c_re, s5_c_im))

    lat, ctx_arr, ctx_block = x, ctx, 0
    for l in range(depth):
        col_major = l % 2 == 1
        last = l == depth - 1
        if col_major:
            lat_in = lat.reshape(n_batch, lat.shape[1] // GRID_W, GRID_W, d_model)
        else:
            lat_in = lat
        z, xc, small, qkv, gg, u5, sg = _inproj(
            lat_in, ctx_arr, ctx_block, mod[l], norm_w[l].reshape(1, -1), w_in_packed[l],
            ssd_conv_w[l], ssd_conv_b[l].reshape(1, -1),
            col_major=col_major, n_batch=n_batch, seq=seq, rows=rows)
        pad_h = lambda a: jnp.pad(a.reshape(1, -1), ((0, 0), (SMALL_DT, LANES - SMALL_DT - a.size)))
        y_ssd = _ssd_scan(xc, small, pad_h(ssd_dt_bias[l]), pad_h(ssd_a_log[l]),
                          jnp.repeat(ssd_d[l], SSD_HEADDIM).reshape(1, -1), seq=seq)

        w_lr = jnp.stack([jnp.pad(gla_w_lr[l, d], ((SMALL_LR + GLA_RANK * d, LANES - SMALL_LR - GLA_RANK * (d + 1)), (0, 0)))
                          for d in (0, 1)])
        o_gla = _gla_scan(qkv, small, w_lr, gla_b_lr[l].reshape(2, 1, -1), seq=seq)

        y_s5 = _s5_scan(u5, s5_ops, seq=seq, layer=l)

        out = _outproj(
            lat_in, ctx_arr, ctx_block, mod[l], y_ssd, z, o_gla, gg, y_s5, u5, sg,
            ssd_norm_w[l].reshape(1, -1), jnp.tile(gla_norm_w[l], GLA_HEADS).reshape(1, -1),
            s5_d[l].reshape(1, -1), glu_w_b[l], s5_glu_b[l].reshape(1, -1),
            w_out_b[l], final_norm_w.reshape(1, -1),
            col_major=col_major, last=last, n_batch=n_batch, seq=seq, rows=rows)
        if last:
            return out.reshape(n_batch, seq, d_model)
        lat, ctx_arr, ctx_block = out, out, n_lat_tiles
```

```python
import functools

import jax
import jax.numpy as jnp
from jax import lax
from jax.experimental import pallas as pl
from jax.experimental.pallas import tpu as pltpu

F32, BF16 = jnp.float32, jnp.bfloat16
STORE = BF16
HIGHEST = lax.Precision.HIGHEST
EPS = 1e-6

D_MODEL = 1024
GRID_W = 64
CTX_LEN = 256
TILE = 256
SSD_BATCH, GLA_BATCH, S5_BATCH = 2, 4, 1
COL_W = 8
LANES = 128
SUB = 8
CONV_COLS = 256
VMEM_LIMIT = 56 * 1024 * 1024

SSD_WIDTH, SSD_HEADS, SSD_HEADDIM, SSD_STATE, SSD_CHUNK = 1024, 16, 64, 128, 128
SSD_CONV_DIM, SSD_CONV_K = 1536, 5
GLA_HEADS, GLA_DK, GLA_DV, GLA_RANK, GLA_TAU, GLA_CHUNK = 4, 64, 128, 16, 16.0, 64
GLA_WIDTH = GLA_HEADS * GLA_DV
S5_WIDTH, S5_GROUP, S5_GROUPS, S5_STATE = 512, 16, 32, 64
S5_T = LANES // S5_GROUP
MIX_WIDTH = SSD_WIDTH + GLA_WIDTH + S5_WIDTH

C_Z, C_XBC, C_SMALL = 0, 1024, 2560
C_QKV, C_GG, C_U5, C_SG, C_END = 2688, 3712, 4224, 4736, 5248
SMALL_DT, SMALL_LR = 0, 32


def _params(semantics):
    return pltpu.CompilerParams(dimension_semantics=semantics, vmem_limit_bytes=VMEM_LIMIT)


def _silu(v):
    return v * jax.nn.sigmoid(v)


def _dot(a, b):
    return jnp.dot(a, b, preferred_element_type=F32)


def _dot_nt(a, b):
    return lax.dot_general(a, b, (((1,), (1,)), ((), ())), preferred_element_type=F32)


def _dot_exact_lhs(m_bf16, v, terms=3):
    out, rest = None, v
    for _ in range(terms):
        piece = rest.astype(BF16)
        rest = rest - piece.astype(F32)
        out = _dot(m_bf16, piece) if out is None else out + _dot(m_bf16, piece)
    return out


def _iota(shape, dim):
    return lax.broadcasted_iota(jnp.int32, shape, dim)


def _interleave(*stages):
    active = list(stages)
    while active:
        for gen in list(active):
            try:
                next(gen)
            except StopIteration:
                active.remove(gen)


def _mod_body(cc_ref, w_ref, b_ref, o_ref):
    s = _silu(cc_ref[...])
    o_ref[0] = jnp.dot(s, w_ref[0], precision=HIGHEST, preferred_element_type=F32) + b_ref[0]


def _modulation(cc, mod_w, mod_b):
    depth, d, three_d = mod_w.shape
    rows = cc.shape[0]
    return pl.pallas_call(
        _mod_body,
        grid=(depth, three_d // d),
        in_specs=[pl.BlockSpec((rows, d), lambda l, n: (0, 0)),
                  pl.BlockSpec((1, d, d), lambda l, n: (l, 0, n)),
                  pl.BlockSpec((1, 1, d), lambda l, n: (l, 0, n))],
        out_specs=pl.BlockSpec((1, rows, d), lambda l, n: (l, 0, n)),
        out_shape=jax.ShapeDtypeStruct((depth, rows, three_d), F32),
        compiler_params=_params(("arbitrary", "arbitrary")),
        name="modulation",
    )(cc, mod_w, mod_b.reshape(depth, 1, three_d))


def _col_base(j, rows):
    cols = TILE // rows
    return (j % (COL_W // cols)) * cols


def _load_tile(lat_ref, ctx_ref, col_major, is_ctx, j):
    if col_major:
        rows = lat_ref.shape[1]
        base = _col_base(j, rows)
        lat = jnp.concatenate([lat_ref[0, :, base + k, :] for k in range(TILE // rows)], axis=0)
    else:
        lat = lat_ref[0]
    if ctx_ref is None:
        return lat
    return jnp.where(is_ctx, ctx_ref[0], lat)


def _token_specs(n_lat_tiles, rows, col_major, with_ctx, ctx_block):
    last = n_lat_tiles - 1
    if col_major:
        per_block = COL_W // (TILE // rows)
        lat = pl.BlockSpec((1, rows, COL_W, D_MODEL), lambda b, j: (b, 0, jnp.minimum(j, last) // per_block, 0))
    else:
        lat = pl.BlockSpec((1, TILE, D_MODEL), lambda b, j: (b, jnp.minimum(j, last), 0))
    if not with_ctx:
        return [lat]
    return [lat, pl.BlockSpec((1, TILE, D_MODEL), lambda b, j: (b, ctx_block, 0))]


def _halo_specs(n_lat_tiles, rows, col_major, seq):
    last = n_lat_tiles - 1
    if col_major:
        cols = TILE // rows
        tile = lambda j: jnp.minimum(j, last)
        prev = pl.BlockSpec((1, SUB, COL_W, D_MODEL),
                            lambda b, j: (b, rows // SUB - 1, jnp.maximum(tile(j) * cols - 1, 0) // COL_W, 0))
        nxt = pl.BlockSpec((1, SUB, COL_W, D_MODEL),
                           lambda b, j: (b, 0, jnp.minimum((tile(j) + 1) * cols, GRID_W - 1) // COL_W, 0))
    else:
        per_tile, n_blk = TILE // SUB, seq // SUB
        tile = lambda j: jnp.minimum(j, last)
        prev = pl.BlockSpec((1, SUB, D_MODEL), lambda b, j: (b, jnp.maximum(tile(j) * per_tile - 1, 0), 0))
        nxt = pl.BlockSpec((1, SUB, D_MODEL), lambda b, j: (b, jnp.minimum((tile(j) + 1) * per_tile, n_blk - 1), 0))
    return [prev, nxt]


def _load_halo(prev_ref, next_ref, col_major, j, rows):
    if col_major:
        cols = TILE // rows
        prev = prev_ref[0, :, jnp.maximum(j * cols - 1, 0) % COL_W, :]
        nxt = next_ref[0, :, jnp.minimum((j + 1) * cols, GRID_W - 1) % COL_W, :]
    else:
        prev, nxt = prev_ref[0], next_ref[0]
    return jnp.concatenate([prev, nxt], axis=0)


def _inproj_body(*refs, col_major, n_batch, n_lat_tiles):
    lat_ref, ctx_ref, prev_ref, next_ref, mod_ref, nw_ref, w_ref, cw_ref, cb_ref = refs[:9]
    z_ref, xc_ref, small_ref, qkv_ref, gg_ref, u5_ref, sg_ref = refs[9:]
    b, j = pl.program_id(0), pl.program_id(1)
    is_ctx = j == n_lat_tiles
    jj = jnp.minimum(j, n_lat_tiles - 1)
    x = _load_tile(lat_ref, ctx_ref, col_major, is_ctx, jj)
    m = mod_ref[pl.ds(jnp.where(is_ctx, n_batch, b), 1), :]
    shift, scale = m[:, :D_MODEL], m[:, D_MODEL:2 * D_MODEL]

    def modulated(v):
        v = v * lax.rsqrt(jnp.mean(v * v, axis=-1, keepdims=True) + EPS) * nw_ref[...]
        return (v * (1.0 + scale) + shift).astype(BF16)

    u = modulated(x)

    u_halo = modulated(_load_halo(prev_ref, next_ref, col_major, jj, lat_ref.shape[1]))
    has_prev = jnp.logical_and(jnp.logical_not(is_ctx), j > 0)
    has_next = jnp.logical_and(jnp.logical_not(is_ctx), j < n_lat_tiles - 1)

    z_ref[0] = _dot(u, w_ref[:, C_Z:C_XBC]).astype(z_ref.dtype)
    for c0 in range(0, SSD_CONV_DIM, CONV_COLS):
        cols = slice(C_XBC + c0, C_XBC + c0 + CONV_COLS)
        halo = _dot(u_halo, w_ref[:, cols])
        ext = jnp.concatenate([jnp.where(has_prev, halo[:SUB], 0.0), _dot(u, w_ref[:, cols]),
                               jnp.where(has_next, halo[SUB:], 0.0)], axis=0)
        acc = jnp.broadcast_to(cb_ref[:, c0:c0 + CONV_COLS], (TILE, CONV_COLS))
        for k in range(SSD_CONV_K):
            off = SUB + k - SSD_CONV_K // 2
            acc = acc + cw_ref[k:k + 1, c0:c0 + CONV_COLS] * ext[off:off + TILE, :]
        xc_ref[0, :, c0:c0 + CONV_COLS] = _silu(acc).astype(xc_ref.dtype)
    small_ref[0] = _dot(u, w_ref[:, C_SMALL:C_QKV])
    qkv_ref[0] = _dot(u, w_ref[:, C_QKV:C_GG]).astype(qkv_ref.dtype)
    gg_ref[0] = _dot(u, w_ref[:, C_GG:C_U5]).astype(gg_ref.dtype)
    u5 = _dot(u, w_ref[:, C_U5:C_SG])
    for k in range(S5_WIDTH // LANES):
        u5_ref[0, k] = u5[:, k * LANES:(k + 1) * LANES]
    sg_ref[0] = _dot(u, w_ref[:, C_SG:C_END]).astype(sg_ref.dtype)


def _inproj(lat, ctx, ctx_block, mod_l, norm_w, w_packed, conv_w, conv_b, *, col_major, n_batch, seq, rows):
    n_lat_tiles = seq // TILE
    lt = seq + CTX_LEN
    tok = lambda width: pl.BlockSpec((1, TILE, width), lambda b, j: (b, j, 0))
    full = lambda a: pl.BlockSpec(a.shape, lambda b, j: (0,) * a.ndim)
    shape = lambda width, dtype=STORE: jax.ShapeDtypeStruct((n_batch, lt, width), dtype)
    return pl.pallas_call(
        functools.partial(_inproj_body, col_major=col_major, n_batch=n_batch, n_lat_tiles=n_lat_tiles),
        grid=(n_batch, n_lat_tiles + 1),
        in_specs=_token_specs(n_lat_tiles, rows, col_major, True, ctx_block)
        + _halo_specs(n_lat_tiles, rows, col_major, seq)
        + [full(mod_l), full(norm_w), full(w_packed), full(conv_w), full(conv_b)],
        out_specs=[tok(1024), tok(SSD_CONV_DIM), tok(LANES), tok(1024), tok(512),
                   pl.BlockSpec((1, S5_WIDTH // LANES, TILE, LANES), lambda b, j: (b, 0, j, 0)),
                   tok(512)],
        out_shape=[shape(1024), shape(SSD_CONV_DIM), shape(LANES, F32), shape(1024), shape(512),
                   jax.ShapeDtypeStruct((n_batch, S5_WIDTH // LANES, lt, LANES), F32), shape(512)],
        compiler_params=_params(("arbitrary", "arbitrary")),
        name="inproj_col" if col_major else "inproj_row",
    )(lat, ctx, lat, lat, mod_l, norm_w, w_packed, conv_w, conv_b)


def _ssd_body(xcf_ref, dtf_ref, xcb_ref, dtb_ref, bias_ref, alog_ref, dsk_ref, yf_ref, yb_ref, stf_ref, stb_ref):
    @pl.when(pl.program_id(1) == 0)
    def _():
        stf_ref[...] = jnp.zeros_like(stf_ref)
        stb_ref[...] = jnp.zeros_like(stb_ref)

    chains = []
    for bi in range(xcf_ref.shape[0]):
        chains.append(_ssd_dir(xcf_ref, dtf_ref, bias_ref, alog_ref, dsk_ref, yf_ref, stf_ref, bi, rev=False))
        chains.append(_ssd_dir(xcb_ref, dtb_ref, bias_ref, alog_ref, dsk_ref, yb_ref, stb_ref, bi, rev=True))
    _interleave(*chains)


def _ssd_dir(xc_ref, dt_ref, bias_ref, alog_ref, dsk_ref, y_ref, st_ref, bi, *, rev):
    t = SSD_CHUNK
    d = 1 if rev else 0
    off = SMALL_DT + SSD_HEADS * d
    dt = jax.nn.softplus(dt_ref[bi] + bias_ref[...])
    da = dt * -jnp.exp(alog_ref[...])
    row, col = _iota((t, t), 0), _iota((t, t), 1)
    tri = row <= col if rev else row >= col
    p = _dot_exact_lhs(jnp.where(tri, 1.0, 0.0).astype(BF16), da)
    total = jnp.sum(da, axis=0, keepdims=True)
    e_tot = jnp.exp(total)
    p_t = p.T
    dt_t = dt.T
    w_t = (dt * jnp.exp(total - p)).T
    lo = _iota((1, LANES), 1) < SSD_HEADDIM
    hi = jnp.logical_not(lo)
    pairs_per_group = SSD_HEADS // 2 // 2
    yield
    for g in range(2):
        bg = xc_ref[bi, :, SSD_WIDTH + g * SSD_STATE:SSD_WIDTH + (g + 1) * SSD_STATE]
        cg = xc_ref[bi, :, SSD_WIDTH + (2 + g) * SSD_STATE:SSD_WIDTH + (3 + g) * SSD_STATE]
        cg_b = cg.astype(BF16)
        cb = _dot_nt(cg_b, bg.astype(BF16))
        bg_t = bg.astype(F32).T
        for q in range(pairs_per_group):
            hp = g * pairs_per_group + q
            h0, h1 = off + 2 * hp, off + 2 * hp + 1
            pb0 = jnp.broadcast_to(p[:, h0:h0 + 1], (t, LANES))
            pb1 = jnp.broadcast_to(p[:, h1:h1 + 1], (t, LANES))
            xp = xc_ref[bi, :, hp * LANES:(hp + 1) * LANES]
            st = st_ref[bi, hp]
            acc = jnp.exp(jnp.where(lo, pb0, pb1)) * _dot(cg_b, st.astype(BF16))
            if not rev:
                acc = acc + dsk_ref[:, hp * LANES:(hp + 1) * LANES] * xp.astype(F32)
            st = jnp.where(lo, e_tot[:, h0:h0 + 1], e_tot[:, h1:h1 + 1]) * st
            for h, pb, sel in ((h0, pb0, lo), (h1, pb1, hi)):
                x_h = jnp.where(sel, xp, jnp.zeros_like(xp)).astype(BF16)
                decay = jnp.where(tri, jnp.exp(pb - p_t[h:h + 1, :]), 0.0)
                acc = acc + _dot((cb * decay * dt_t[h:h + 1, :]).astype(BF16), x_h)
                st = st + _dot((bg_t * w_t[h:h + 1, :]).astype(BF16), x_h)
            y_ref[bi, :, hp * LANES:(hp + 1) * LANES] = acc.astype(y_ref.dtype)
            st_ref[bi, hp] = st
            yield


def _ssd_scan(xc, dt, dt_bias, a_log, dsk, *, seq):
    n_batch, total, _ = xc.shape
    nc = total // SSD_CHUNK
    lat_c = seq // SSD_CHUNK
    ctx_c = nc - lat_c
    fwd = lambda i: jnp.where(i < ctx_c, lat_c + i, i - ctx_c)
    bwd = lambda i: nc - 1 - i
    full = lambda a: pl.BlockSpec(a.shape, lambda b, i: (0,) * a.ndim)
    nb = SSD_BATCH if n_batch % SSD_BATCH == 0 else 1
    xc_spec =lambda chunk: pl.BlockSpec((nb, SSD_CHUNK, SSD_CONV_DIM), lambda b, i: (b, chunk(i), 0))
    dt_spec = lambda chunk: pl.BlockSpec((nb, SSD_CHUNK, LANES), lambda b, i: (b, chunk(i), 0))
    y_spec = lambda chunk: pl.BlockSpec((nb, SSD_CHUNK, SSD_WIDTH), lambda b, i: (b, chunk(i), 0))
    y_shape = jax.ShapeDtypeStruct((n_batch, total, SSD_WIDTH), STORE)
    return pl.pallas_call(
        _ssd_body,
        grid=(n_batch // nb, nc),
        in_specs=[xc_spec(fwd), dt_spec(fwd), xc_spec(bwd), dt_spec(bwd),
                  full(dt_bias), full(a_log), full(dsk)],
        out_specs=[y_spec(fwd), y_spec(bwd)],
        out_shape=[y_shape, y_shape],
        scratch_shapes=[pltpu.VMEM((nb, SSD_HEADS // 2, SSD_STATE, LANES), F32)] * 2,
        compiler_params=_params(("arbitrary", "arbitrary")),
        name="ssd_scan",
    )(xc, dt, xc, dt, dt_bias, a_log, dsk)


def _tile_index(d, i, n_lat_tiles):
    return jnp.where(i == 0, n_lat_tiles, jnp.where(d == 0, i - 1, n_lat_tiles - i))


def _gla_body(qkvf_ref, lrf_ref, qkvb_ref, lrb_ref, wlr_ref, blr_ref, of_ref, ob_ref, sf_ref, sb_ref):
    @pl.when(pl.program_id(1) == 0)
    def _():
        sf_ref[...] = jnp.zeros_like(sf_ref)
        sb_ref[...] = jnp.zeros_like(sb_ref)

    chains = []
    for bi in range(qkvf_ref.shape[0]):
        chains.append(_gla_dir(qkvf_ref, lrf_ref, wlr_ref, blr_ref, of_ref, sf_ref, bi, rev=False))
        chains.append(_gla_dir(qkvb_ref, lrb_ref, wlr_ref, blr_ref, ob_ref, sb_ref, bi, rev=True))
    _interleave(*chains)


def _gla_dir(qkv_ref, lr_ref, wlr_ref, blr_ref, o_ref, s_ref, bi, *, rev):
    d = 1 if rev else 0
    t = GLA_CHUNK
    n_chunks = TILE // t
    kw = GLA_HEADS * GLA_DK
    row, col = _iota((TILE, TILE), 0), _iota((TILE, TILE), 1)
    same = row // t == col // t
    tri = jnp.where(same, row - col, 1 if rev else -1)
    tri = tri <= 0 if rev else tri >= 0
    lo = _iota((1, LANES), 1) < GLA_DK
    hi = jnp.logical_not(lo)
    eye = _iota((LANES, LANES), 0) == _iota((LANES, LANES), 1)
    row_lo = _iota((LANES, 1), 0) < GLA_DK

    logit = jnp.dot(lr_ref[bi], wlr_ref[d], precision=HIGHEST, preferred_element_type=F32) + blr_ref[d]
    yield
    g = jax.nn.log_sigmoid(logit) / GLA_TAU
    p = _dot_exact_lhs(jnp.where(tri, 1.0, 0.0).astype(BF16), g, terms=2)
    total = jnp.concatenate(
        [jnp.broadcast_to(p[(c * t if rev else (c + 1) * t - 1):(c * t + 1 if rev else (c + 1) * t), :], (t, kw))
         for c in range(n_chunks)], axis=0)
    yield
    q = qkv_ref[bi, :, 0:kw].astype(F32) * GLA_DK ** -0.5
    k = qkv_ref[bi, :, kw:2 * kw].astype(F32)
    qe = q * jnp.exp(p)
    ke = k * jnp.exp(-p)
    kd = k * jnp.exp(total - p)
    e_tot = jnp.exp(total)
    order = range(n_chunks - 1, -1, -1) if rev else range(n_chunks)
    pairs = range(GLA_HEADS // 2)
    lanes = lambda pr: slice(pr * LANES, (pr + 1) * LANES)
    qms, intra, vs = {}, {}, {}
    for pr in pairs:
        ke_p = ke[:, lanes(pr)].astype(BF16)
        for e, sel in ((0, lo), (1, hi)):
            h = 2 * pr + e
            qms[h] = jnp.where(sel, qe[:, lanes(pr)], 0.0).astype(BF16)
            vs[h] = qkv_ref[bi, :, 2 * kw + h * GLA_DV:2 * kw + (h + 1) * GLA_DV].astype(BF16)
            attn = jnp.where(tri, _dot_nt(qms[h], ke_p), 0.0).astype(BF16)
            yield
            intra[h] = _dot(attn, vs[h])
            yield
    upd, dec = {}, {}
    for pr in pairs:
        for c in order:
            rows = slice(c * t, (c + 1) * t)
            kd_t = kd[rows, lanes(pr)].T.astype(BF16)
            upd[pr, c] = jnp.where(row_lo, _dot(kd_t, vs[2 * pr][rows]), _dot(kd_t, vs[2 * pr + 1][rows]))
            dec[pr, c] = jnp.sum(jnp.where(eye, e_tot[c * t:c * t + 1, lanes(pr)], 0.0), axis=1, keepdims=True)
        yield
    for pr in pairs:
        s = s_ref[bi, pr]
        for c in order:
            rows = slice(c * t, (c + 1) * t)
            s_b = s.astype(BF16)
            for h in (2 * pr, 2 * pr + 1):
                o_ref[bi, rows, h * GLA_DV:(h + 1) * GLA_DV] = (intra[h][rows] + _dot(qms[h][rows], s_b)).astype(o_ref.dtype)
            s = dec[pr, c] * s + upd[pr, c]
            yield
        s_ref[bi, pr] = s


def _gla_scan(qkv, lr, w_lr, b_lr, *, seq):
    n_batch, total, _ = qkv.shape
    n_lat_tiles = seq // TILE
    tile = lambda d: (lambda i: _tile_index(d, i, n_lat_tiles))
    full = lambda a: pl.BlockSpec(a.shape, lambda b, i: (0,) * a.ndim)
    nb = GLA_BATCH if n_batch % GLA_BATCH == 0 else 1
    qkv_spec =lambda d: pl.BlockSpec((nb, TILE, 1024), lambda b, i: (b, tile(d)(i), 0))
    lr_spec = lambda d: pl.BlockSpec((nb, TILE, LANES), lambda b, i: (b, tile(d)(i), 0))
    o_spec = lambda d: pl.BlockSpec((nb, TILE, GLA_WIDTH), lambda b, i: (b, tile(d)(i), 0))
    o_shape = jax.ShapeDtypeStruct((n_batch, total, GLA_WIDTH), STORE)
    return pl.pallas_call(
        _gla_body,
        grid=(n_batch // nb, n_lat_tiles + 1),
        in_specs=[qkv_spec(0), lr_spec(0), qkv_spec(1), lr_spec(1), full(w_lr), full(b_lr)],
        out_specs=[o_spec(0), o_spec(1)],
        out_shape=[o_shape, o_shape],
        scratch_shapes=[pltpu.VMEM((nb, GLA_HEADS // 2, LANES, GLA_DV), F32)] * 2,
        compiler_params=_params(("arbitrary", "arbitrary")),
        name="gla_scan",
    )(qkv, lr, qkv, lr, w_lr, b_lr)


S5_ROWS = TILE // S5_T
SLABS_IN_FLIGHT = 4


def _s5_prep_body(ls_ref, lrr_ref, lir_ref, lrc_ref, lic_ref, p1_ref, p2_ref, q1_ref, q2_ref,
                  g_ref, win_ref, wout_ref, ar_ref, ai_ref):
    fwd = pl.program_id(0) % 2 == 0
    n = LANES
    step = jnp.exp(ls_ref[0, 0])
    lo = _iota((1, n), 1) < S5_STATE
    top = _iota((n, 1), 0) < S5_STATE

    lre, lim = lrr_ref[0, 0], lir_ref[0, 0]
    lr, ang = lre * step, lim * step
    er = jnp.exp(lr)
    nr, ni = er * jnp.cos(ang) - 1.0, er * jnp.sin(ang)
    den = lre * lre + lim * lim
    cr, ci = (nr * lre + ni * lim) / den, (ni * lre - nr * lim) / den
    p1, p2 = p1_ref[0, 0], p2_ref[0, 0]
    bbr, bbi = cr * p1 - ci * p2, cr * p2 + ci * p1
    s_idx = _iota((n, 1), 0) // S5_GROUP
    pos_s = jnp.where(fwd, s_idx, S5_T - 1 - s_idx)
    xs = (S5_T - 1 - pos_s).astype(F32)
    e = jnp.exp(xs * lr)
    l_re, l_im = e * jnp.cos(xs * ang), e * jnp.sin(xs * ang)
    w_in = jnp.where(lo, bbr * l_re - bbi * l_im, bbr * l_im + bbi * l_re)
    win_ref[0, 0] = w_in.astype(BF16)

    lrc, angc = lrc_ref[0, 0] * step, lic_ref[0, 0] * step
    q1, q2 = q1_ref[0, 0], q2_ref[0, 0]
    t_idx = _iota((1, n), 1) // S5_GROUP
    pos_t = jnp.where(fwd, t_idx, S5_T - 1 - t_idx)

    def out_op(xt):
        ex = jnp.exp(lrc * xt)
        o_re, o_im = ex * jnp.cos(angc * xt), ex * jnp.sin(angc * xt)
        return jnp.where(top, q1 * o_re - q2 * o_im, -(q1 * o_im + q2 * o_re))

    wout_ref[0, 0] = out_op((pos_t + 1).astype(F32)).astype(BF16)
    g = jnp.dot(w_in, out_op((pos_t - (S5_T - 1)).astype(F32)), precision=HIGHEST,
                preferred_element_type=F32)
    g_ref[0, 0] = jnp.where(pos_t >= pos_s, g, 0.0).astype(BF16)

    r_idx = _iota((S5_ROWS, 1), 0)
    xe = (S5_T * (jnp.where(fwd, r_idx, S5_ROWS - 1 - r_idx) + 1)).astype(F32)
    ee = jnp.exp(xe * lr)
    a_im = ee * jnp.sin(xe * ang)
    ar_ref[0, 0] = ee * jnp.cos(xe * ang)
    ai_ref[0, 0] = jnp.where(lo, -a_im, a_im)


def _s5_prep(lam_re, lam_im, log_step, b_re, b_im, c_re, c_im):
    n = LANES
    ld = 2 * lam_re.shape[0]
    dup_l = lambda a: jnp.concatenate([a, a], axis=-1)
    dup_r = lambda a: jnp.concatenate([a, a], axis=-2)
    merge = lambda a: a.reshape((ld,) + a.shape[2:])
    lrr, lir = dup_l(merge(lam_re))[:, :, None, :], dup_l(merge(lam_im))[:, :, None, :]
    lrc, lic = lrr.reshape(ld, S5_GROUPS, n, 1), lir.reshape(ld, S5_GROUPS, n, 1)
    ls = log_step.reshape(ld, S5_GROUPS, 1, 1)
    tile_rows = lambda a: dup_l(jnp.tile(jnp.swapaxes(a, -1, -2), (1, 1, S5_T, 1)))
    p1, p2 = tile_rows(b_re), tile_rows(b_im)
    tile_cols = lambda a: dup_r(jnp.tile(jnp.swapaxes(merge(a), -1, -2), (1, 1, 1, S5_T)))
    q1, q2 = tile_cols(c_re), tile_cols(c_im)
    dg = lambda *blk: pl.BlockSpec((1, 1) + blk, lambda d, g: (d, g, 0, 0))
    gg = lambda *blk: pl.BlockSpec((1, 1) + blk, lambda d, g: (d // 2, g, 0, 0))
    out = lambda rows, dtype: jax.ShapeDtypeStruct((ld, S5_GROUPS, rows, n), dtype)
    return pl.pallas_call(
        _s5_prep_body,
        grid=(ld, S5_GROUPS),
        in_specs=[dg(1, 1), dg(1, n), dg(1, n), dg(n, 1), dg(n, 1), gg(n, n), gg(n, n), dg(n, n), dg(n, n)],
        out_specs=[dg(n, n), dg(n, n), dg(n, n), dg(S5_ROWS, n), dg(S5_ROWS, n)],
        out_shape=[out(n, BF16), out(n, BF16), out(n, BF16), out(S5_ROWS, F32), out(S5_ROWS, F32)],
        compiler_params=_params(("arbitrary", "arbitrary")),
        name="s5_prep",
    )(ls, lrr, lir, lrc, lic, p1, p2, q1, q2)


def _s5_pair_ops(g, win, wout, ar, ai):
    hs = S5_STATE
    cat = jnp.concatenate
    pair = lambda a: (a[:, 0::2], a[:, 1::2])
    g0, g1 = pair(g)
    i0, i1 = pair(win)
    o0, o1 = pair(wout)
    z = jnp.zeros_like(g0)
    zh, zr = z[..., :hs], z[..., :hs, :]
    wa = cat([cat([g0, z, i0[..., :hs], zh, i0[..., hs:], zh], -1),
              cat([z, g1, zh, i1[..., :hs], zh, i1[..., hs:]], -1)], -2)
    wo = cat([cat([o0[..., :hs, :], zr], -1), cat([zr, o1[..., :hs, :]], -1),
              cat([o0[..., hs:, :], zr], -1), cat([zr, o1[..., hs:, :]], -1)], -2)
    a0, a1 = pair(ar)
    b0, b1 = pair(ai)
    return wa, wo, cat([a0[..., :hs], a1[..., :hs]], -1), cat([b0[..., hs:], b1[..., hs:]], -1)


def _block_transpose(arrs):
    n = len(arrs)
    block = _iota((1, LANES), 1) // S5_GROUP
    arrs = list(arrs)
    s = n // 2
    while s >= 1:
        upper = (block // s) % 2 == 1
        nxt = list(arrs)
        for i in range(n):
            if (i // s) % 2 == 0:
                a, b = arrs[i], arrs[i + s]
                nxt[i] = jnp.where(upper, pltpu.roll(b, s * S5_GROUP, 1), a)
                nxt[i + s] = jnp.where(upper, b, pltpu.roll(a, LANES - s * S5_GROUP, 1))
        arrs = nxt
        s //= 2
    return arrs


def _s5_body(uf_ref, ub_ref, wa_ref, wo_ref, are_ref, aim_ref, yf_ref, yb_ref, *h_refs):
    @pl.when(pl.program_id(1) == 0)
    def _():
        for h_ref in h_refs:
            h_ref[...] = jnp.zeros_like(h_ref)

    ops = (wa_ref, wo_ref, are_ref, aim_ref)
    slabs = []
    for bi in range(uf_ref.shape[0]):
        slabs.append(_s5_dir(uf_ref, *ops, yf_ref, *h_refs[:2], bi, rev=False))
        slabs.append(_s5_dir(ub_ref, *ops, yb_ref, *h_refs[2:], bi, rev=True))
    for k0 in range(0, S5_WIDTH // LANES, SLABS_IN_FLIGHT):
        _interleave(*[slab(k) for k in range(k0, k0 + SLABS_IN_FLIGHT) for slab in slabs])


def _s5_dir(u_ref, wa_ref, wo_ref, are_ref, aim_ref, y_ref, hre_ref, him_ref, bi, *, rev):
    d = 1 if rev else 0
    rws = S5_ROWS
    row = _iota((rws, 1), 0)
    pos = rws - 1 - row if rev else row
    per_slab = LANES // S5_GROUP
    pairs = per_slab // 2

    def earlier(v, k):
        return pltpu.roll(v, rws - k if rev else k, 0)

    def slab(k):
        parts = [u_ref[bi, k, pl.ds(t, rws, stride=S5_T), :] for t in range(S5_T)]
        h_all_re, h_all_im = hre_ref[bi, k], him_ref[bi, k]
        xs = _block_transpose(parts)
        yield
        first = []
        for q in range(pairs):
            x2 = jnp.concatenate([xs[2 * q], xs[2 * q + 1]], axis=1).astype(BF16)
            first.append(_dot(x2, wa_ref[d, k * pairs + q]))
            yield
        ys, new_re, new_im = [], [], []
        for q in range(pairs):
            pr = k * pairs + q
            res = first[q]
            y_intra, h_re, h_im = res[:, :2 * LANES], res[:, 2 * LANES:3 * LANES], res[:, 3 * LANES:]
            span = 1
            while span < SUB:
                idx = rws - span if rev else span - 1
                a_re, a_im = are_ref[d, pr, idx:idx + 1, :], aim_ref[d, pr, idx:idx + 1, :]
                s_re = jnp.where(pos % SUB >= span, earlier(h_re, span), 0.0)
                s_im = jnp.where(pos % SUB >= span, earlier(h_im, span), 0.0)
                h_re, h_im = h_re + (a_re * s_re - a_im * s_im), h_im + (a_re * s_im + a_im * s_re)
                span *= 2
            c_re, c_im = h_all_re[SUB * q:SUB * q + 1, :], h_all_im[SUB * q:SUB * q + 1, :]
            in_re, in_im = jnp.broadcast_to(c_re, (rws, LANES)), jnp.broadcast_to(c_im, (rws, LANES))
            first_block = slice(rws - SUB, rws) if rev else slice(0, SUB)
            t_re, t_im = are_ref[d, pr, first_block, :], aim_ref[d, pr, first_block, :]
            blocks_re, blocks_im = {}, {}
            for v in (range(rws // SUB - 1, -1, -1) if rev else range(rws // SUB)):
                blk = slice(SUB * v, SUB * (v + 1))
                b_re = h_re[blk] + (t_re * c_re - t_im * c_im)
                b_im = h_im[blk] + (t_re * c_im + t_im * c_re)
                blocks_re[v], blocks_im[v] = b_re, b_im
                edge = 0 if rev else SUB - 1
                c_re, c_im = b_re[edge:edge + 1, :], b_im[edge:edge + 1, :]
            h_re = jnp.concatenate([blocks_re[v] for v in range(rws // SUB)], axis=0)
            h_im = jnp.concatenate([blocks_im[v] for v in range(rws // SUB)], axis=0)
            p_re = jnp.where(pos == 0, in_re, earlier(h_re, 1))
            p_im = jnp.where(pos == 0, in_im, earlier(h_im, 1))
            new_re.append(jnp.broadcast_to(c_re, (SUB, LANES)))
            new_im.append(jnp.broadcast_to(c_im, (SUB, LANES)))
            y2 = y_intra + _dot(jnp.concatenate([p_re, p_im], axis=1).astype(BF16), wo_ref[d, pr])
            ys += [y2[:, :LANES], y2[:, LANES:]]
            yield
        hre_ref[bi, k] = jnp.concatenate(new_re, axis=0)
        him_ref[bi, k] = jnp.concatenate(new_im, axis=0)
        for t, out in enumerate(_block_transpose(ys)):
            y_ref[bi, k, pl.ds(t, rws, stride=S5_T), :] = out

    return slab


def _s5_scan(u5, ops, *, seq, layer):
    n_batch, slabs, total, _ = u5.shape
    n_lat_tiles = seq // TILE
    nb = S5_BATCH if n_batch % S5_BATCH == 0 else 1
    tok =lambda d: pl.BlockSpec((nb, slabs, TILE, LANES), lambda b, i: (b, 0, _tile_index(d, i, n_lat_tiles), 0))
    full = lambda a: pl.BlockSpec((2,) + a.shape[1:], lambda b, i: (layer, 0, 0, 0))
    pairs_per_slab = LANES // S5_GROUP // 2
    y_shape = jax.ShapeDtypeStruct((n_batch, slabs, total, LANES), F32)
    return pl.pallas_call(
        _s5_body,
        grid=(n_batch // nb, n_lat_tiles + 1),
        in_specs=[tok(0), tok(1)] + [full(a) for a in ops],
        out_specs=[tok(0), tok(1)],
        out_shape=[y_shape, y_shape],
        scratch_shapes=[pltpu.VMEM((nb, slabs, SUB * pairs_per_slab, LANES), F32)] * 4,
        compiler_params=_params(("arbitrary", "arbitrary")),
        name="s5_scan",
    )(u5, u5, *ops)


def _outproj_body(*refs, col_major, last, n_batch, n_lat_tiles):
    n_tok = 1 if last else 2
    lat_ref = refs[0]
    ctx_ref = None if last else refs[1]
    (mod_ref, yf_ref, yb_ref, z_ref, of_ref, ob_ref, gg_ref, sf_ref, sb_ref, u5_ref, sg_ref,
     snw_ref, gnw_ref, s5d_ref, gluw_ref, glub_ref, wout_ref, fnw_ref, o_ref) = refs[n_tok:]
    b, j = pl.program_id(0), pl.program_id(1)
    is_ctx = j == n_lat_tiles
    res = _load_tile(lat_ref, ctx_ref, col_major, is_ctx, jnp.minimum(j, n_lat_tiles - 1))
    gate = mod_ref[pl.ds(jnp.where(is_ctx, n_batch, b), 1), 2 * D_MODEL:3 * D_MODEL]

    def rms(v):
        return v * lax.rsqrt(jnp.mean(v * v, axis=-1, keepdims=True) + EPS)

    y = (yf_ref[0].astype(F32) + yb_ref[0].astype(F32)) * _silu(z_ref[0].astype(F32))
    half = SSD_WIDTH // 2
    ssd = jnp.concatenate([rms(y[:, :half]), rms(y[:, half:])], axis=1) * snw_ref[...]
    out = _dot(ssd.astype(BF16), wout_ref[0:SSD_WIDTH, :])

    o = of_ref[0].astype(F32) + ob_ref[0].astype(F32)
    gla = jnp.concatenate([rms(o[:, h * GLA_DV:(h + 1) * GLA_DV]) for h in range(GLA_HEADS)], axis=1)
    gla = gla * gnw_ref[...] * _silu(gg_ref[0].astype(F32))
    out = out + _dot(gla.astype(BF16), wout_ref[SSD_WIDTH:SSD_WIDTH + GLA_WIDTH, :])

    slabs = range(S5_WIDTH // LANES)
    y5 = jnp.concatenate([sf_ref[0, k] + sb_ref[0, k] for k in slabs], axis=1)
    u5 = jnp.concatenate([u5_ref[0, k] for k in slabs], axis=1)
    y5 = jax.nn.gelu(y5 + s5d_ref[...] * u5)
    pr = _dot(y5.astype(BF16), gluw_ref[...]) + glub_ref[...]
    s5 = pr[:, :S5_WIDTH] * jax.nn.sigmoid(pr[:, S5_WIDTH:]) * _silu(sg_ref[0].astype(F32))
    out = out + _dot(s5.astype(BF16), wout_ref[SSD_WIDTH + GLA_WIDTH:, :])

    h_new = res + gate * out
    if last:
        h_new = rms(h_new) * fnw_ref[...]
    if col_major:
        rows = o_ref.shape[1]
        base = _col_base(j, rows)
        for k in range(TILE // rows):
            o_ref[0, :, base + k, :] = h_new[k * rows:(k + 1) * rows, :]
    else:
        o_ref[0] = h_new


def _outproj(lat, ctx, ctx_block, mod_l, y_ssd, z, o_gla, gg, y_s5, u5, sg, ssd_norm_w, gla_norm_w, s5_d,
             glu_w, glu_b, w_out, final_norm_w, *, col_major, last, n_batch, seq, rows):
    n_lat_tiles = seq // TILE
    n_tiles = n_lat_tiles if last else n_lat_tiles + 1
    tok = lambda width: pl.BlockSpec((1, TILE, width), lambda b, j: (b, j, 0))
    both = lambda width: [tok(width), tok(width)]
    slabs = S5_WIDTH // LANES
    s5_tok = pl.BlockSpec((1, slabs, TILE, LANES), lambda b, j: (b, 0, j, 0))
    full = lambda a: pl.BlockSpec(a.shape, lambda b, j: (0,) * a.ndim)
    small = [ssd_norm_w, gla_norm_w, s5_d, glu_w, glu_b, w_out, final_norm_w]
    tokens = (lat,) if last else (lat, ctx)
    if col_major:
        per_block = COL_W // (TILE // rows)
        out_spec = pl.BlockSpec((1, rows, COL_W, D_MODEL), lambda b, j: (b, 0, j // per_block, 0))
        out_shape = jax.ShapeDtypeStruct((n_batch, rows, seq // rows, D_MODEL), F32)
    else:
        out_spec = tok(D_MODEL)
        out_shape = jax.ShapeDtypeStruct((n_batch, n_tiles * TILE, D_MODEL), F32)
    return pl.pallas_call(
        functools.partial(_outproj_body, col_major=col_major, last=last, n_batch=n_batch,
                          n_lat_tiles=n_lat_tiles),
        grid=(n_batch, n_tiles),
        in_specs=_token_specs(n_lat_tiles, rows, col_major, not last, ctx_block)
        + [full(mod_l)] + both(SSD_WIDTH) + [tok(SSD_WIDTH)] + both(GLA_WIDTH) + [tok(GLA_WIDTH)]
        + [s5_tok, s5_tok, s5_tok, tok(S5_WIDTH)]
        + [full(a) for a in small],
        out_specs=out_spec,
        out_shape=out_shape,
        compiler_params=_params(("arbitrary", "arbitrary")),
        name="outproj_col" if col_major else "outproj_row",
    )(*tokens, mod_l, *y_ssd, z, *o_gla, gg, *y_s5, u5, sg, *small)


def _pack_w_in(w):
    sizes = (1024, SSD_CONV_DIM, 32, 256, 256, 512, 512, 32, 512, 512)
    parts, start = [], 0
    for size in sizes:
        parts.append(w[..., start:start + size])
        start += size
    z, xbc, dt, q, k, v, gg, lr, u5, sg = parts
    small = jnp.concatenate([dt, lr], axis=-1)
    small = jnp.pad(small, ((0, 0),) * (w.ndim - 1) + ((0, LANES - small.shape[-1]),))
    return jnp.concatenate([z, xbc, small, q, k, v, gg, u5, sg], axis=-1).astype(BF16)


def kernel(x, c, ctx, c_ctx, norm_w, mod_w, mod_b, w_in, w_out, ssd_conv_w, ssd_conv_b, ssd_a_log, ssd_dt_bias, ssd_d, ssd_norm_w, gla_w_lr, gla_b_lr, gla_norm_w, s5_lam_re, s5_lam_im, s5_log_step, s5_b_re, s5_b_im, s5_c_re, s5_c_im, s5_d, s5_glu_w, s5_glu_b, final_norm_w):
    n_batch, seq, d_model = x.shape
    depth = norm_w.shape[0]
    rows = seq // GRID_W
    assert d_model == D_MODEL and ctx.shape[1] == CTX_LEN == TILE
    assert seq % TILE == 0 and TILE % rows == 0
    assert depth == 2, "layer 0 row-major with context update, layer 1 column-major and last"
    n_lat_tiles = seq // TILE

    mod_rows = -(-(n_batch + 1) // 8) * 8
    cc = jnp.zeros((mod_rows, d_model), F32).at[:n_batch].set(c).at[n_batch].set(c_ctx)
    mod = _modulation(cc, mod_w, mod_b)

    w_in_packed, w_out_b, glu_w_b = _pack_w_in(w_in), w_out.astype(BF16), s5_glu_w.astype(BF16)
    s5_ops = _s5_pair_ops(*_s5_prep(s5_lam_re, s5_lam_im, s5_log_step, s5_b_re, s5_b_im, s5_c_re, s5_c_im))

    lat, ctx_arr, ctx_block = x, ctx, 0
    for l in range(depth):
        col_major = l % 2 == 1
        last = l == depth - 1
        if col_major:
            lat_in = lat.reshape(n_batch, lat.shape[1] // GRID_W, GRID_W, d_model)
        else:
            lat_in = lat
        z, xc, small, qkv, gg, u5, sg = _inproj(
            lat_in, ctx_arr, ctx_block, mod[l], norm_w[l].reshape(1, -1), w_in_packed[l],
            ssd_conv_w[l], ssd_conv_b[l].reshape(1, -1),
            col_major=col_major, n_batch=n_batch, seq=seq, rows=rows)
        pad_h = lambda a: jnp.pad(a.reshape(1, -1), ((0, 0), (SMALL_DT, LANES - SMALL_DT - a.size)))
        y_ssd = _ssd_scan(xc, small, pad_h(ssd_dt_bias[l]), pad_h(ssd_a_log[l]),
                          jnp.repeat(ssd_d[l], SSD_HEADDIM).reshape(1, -1), seq=seq)

        w_lr = jnp.stack([jnp.pad(gla_w_lr[l, d], ((SMALL_LR + GLA_RANK * d, LANES - SMALL_LR - GLA_RANK * (d + 1)), (0, 0)))
                          for d in (0, 1)])
        o_gla = _gla_scan(qkv, small, w_lr, gla_b_lr[l].reshape(2, 1, -1), seq=seq)

        y_s5 = _s5_scan(u5, s5_ops, seq=seq, layer=l)

        out = _outproj(
            lat_in, ctx_arr, ctx_block, mod[l], y_ssd, z, o_gla, gg, y_s5, u5, sg,
            ssd_norm_w[l].reshape(1, -1), jnp.tile(gla_norm_w[l], GLA_HEADS).reshape(1, -1),
            s5_d[l].reshape(1, -1), glu_w_b[l], s5_glu_b[l].reshape(1, -1),
            w_out_b[l], final_norm_w.reshape(1, -1),
            col_major=col_major, last=last, n_batch=n_batch, seq=seq, rows=rows)
        if last:
            return out.reshape(n_batch, seq, d_model)
        lat, ctx_arr, ctx_block = out, out, n_lat_tiles
```

```python
import functools

import jax
import jax.numpy as jnp
from jax import lax
from jax.experimental import pallas as pl
from jax.experimental.pallas import tpu as pltpu

F32, BF16 = jnp.float32, jnp.bfloat16
STORE = BF16
HIGHEST = lax.Precision.HIGHEST
EPS = 1e-6

D_MODEL = 1024
GRID_W = 64
CTX_LEN = 256
TILE = 256
SSD_BATCH, GLA_BATCH, S5_BATCH = 2, 4, 1
COL_W = 8
LANES = 128
SUB = 8
CONV_COLS = 256
VMEM_LIMIT = 56 * 1024 * 1024

SSD_WIDTH, SSD_HEADS, SSD_HEADDIM, SSD_STATE, SSD_CHUNK = 1024, 16, 64, 128, 128
SSD_CONV_DIM, SSD_CONV_K = 1536, 5
GLA_HEADS, GLA_DK, GLA_DV, GLA_RANK, GLA_TAU, GLA_CHUNK = 4, 64, 128, 16, 16.0, 64
GLA_WIDTH = GLA_HEADS * GLA_DV
S5_WIDTH, S5_GROUP, S5_GROUPS, S5_STATE = 512, 16, 32, 64
S5_T = LANES // S5_GROUP
MIX_WIDTH = SSD_WIDTH + GLA_WIDTH + S5_WIDTH

C_Z, C_XBC, C_SMALL = 0, 1024, 2560
C_QKV, C_GG, C_U5, C_SG, C_END = 2688, 3712, 4224, 4736, 5248
SMALL_DT, SMALL_LR = 0, 32


def _params(semantics):
    return pltpu.CompilerParams(dimension_semantics=semantics, vmem_limit_bytes=VMEM_LIMIT)


def _silu(v):
    return v * jax.nn.sigmoid(v)


def _dot(a, b):
    return jnp.dot(a, b, preferred_element_type=F32)


def _dot_nt(a, b):
    return lax.dot_general(a, b, (((1,), (1,)), ((), ())), preferred_element_type=F32)


def _dot_exact_lhs(m_bf16, v, terms=3):
    out, rest = None, v
    for _ in range(terms):
        piece = rest.astype(BF16)
        rest = rest - piece.astype(F32)
        out = _dot(m_bf16, piece) if out is None else out + _dot(m_bf16, piece)
    return out


def _iota(shape, dim):
    return lax.broadcasted_iota(jnp.int32, shape, dim)


def _interleave(*stages):
    active = list(stages)
    while active:
        for gen in list(active):
            try:
                next(gen)
            except StopIteration:
                active.remove(gen)


def _mod_body(cc_ref, w_ref, b_ref, o_ref):
    s = _silu(cc_ref[...])
    o_ref[0] = jnp.dot(s, w_ref[0], precision=HIGHEST, preferred_element_type=F32) + b_ref[0]


def _modulation(cc, mod_w, mod_b):
    depth, d, three_d = mod_w.shape
    rows = cc.shape[0]
    return pl.pallas_call(
        _mod_body,
        grid=(depth, three_d // d),
        in_specs=[pl.BlockSpec((rows, d), lambda l, n: (0, 0)),
                  pl.BlockSpec((1, d, d), lambda l, n: (l, 0, n)),
                  pl.BlockSpec((1, 1, d), lambda l, n: (l, 0, n))],
        out_specs=pl.BlockSpec((1, rows, d), lambda l, n: (l, 0, n)),
        out_shape=jax.ShapeDtypeStruct((depth, rows, three_d), F32),
        compiler_params=_params(("arbitrary", "arbitrary")),
        name="modulation",
    )(cc, mod_w, mod_b.reshape(depth, 1, three_d))


def _col_base(j, rows):
    cols = TILE // rows
    return (j % (COL_W // cols)) * cols


def _load_tile(lat_ref, ctx_ref, col_major, is_ctx, j):
    if col_major:
        rows = lat_ref.shape[1]
        base = _col_base(j, rows)
        lat = jnp.concatenate([lat_ref[0, :, base + k, :] for k in range(TILE // rows)], axis=0)
    else:
        lat = lat_ref[0]
    if ctx_ref is None:
        return lat
    return jnp.where(is_ctx, ctx_ref[0], lat)


def _token_specs(n_lat_tiles, rows, col_major, with_ctx, ctx_block):
    last = n_lat_tiles - 1
    if col_major:
        per_block = COL_W // (TILE // rows)
        lat = pl.BlockSpec((1, rows, COL_W, D_MODEL), lambda b, j: (b, 0, jnp.minimum(j, last) // per_block, 0))
    else:
        lat = pl.BlockSpec((1, TILE, D_MODEL), lambda b, j: (b, jnp.minimum(j, last), 0))
    if not with_ctx:
        return [lat]
    return [lat, pl.BlockSpec((1, TILE, D_MODEL), lambda b, j: (b, ctx_block, 0))]


def _halo_specs(n_lat_tiles, rows, col_major, seq):
    last = n_lat_tiles - 1
    if col_major:
        cols = TILE // rows
        tile = lambda j: jnp.minimum(j, last)
        prev = pl.BlockSpec((1, SUB, COL_W, D_MODEL),
                            lambda b, j: (b, rows // SUB - 1, jnp.maximum(tile(j) * cols - 1, 0) // COL_W, 0))
        nxt = pl.BlockSpec((1, SUB, COL_W, D_MODEL),
                           lambda b, j: (b, 0, jnp.minimum((tile(j) + 1) * cols, GRID_W - 1) // COL_W, 0))
    else:
        per_tile, n_blk = TILE // SUB, seq // SUB
        tile = lambda j: jnp.minimum(j, last)
        prev = pl.BlockSpec((1, SUB, D_MODEL), lambda b, j: (b, jnp.maximum(tile(j) * per_tile - 1, 0), 0))
        nxt = pl.BlockSpec((1, SUB, D_MODEL), lambda b, j: (b, jnp.minimum((tile(j) + 1) * per_tile, n_blk - 1), 0))
    return [prev, nxt]


def _load_halo(prev_ref, next_ref, col_major, j, rows):
    if col_major:
        cols = TILE // rows
        prev = prev_ref[0, :, jnp.maximum(j * cols - 1, 0) % COL_W, :]
        nxt = next_ref[0, :, jnp.minimum((j + 1) * cols, GRID_W - 1) % COL_W, :]
    else:
        prev, nxt = prev_ref[0], next_ref[0]
    return jnp.concatenate([prev, nxt], axis=0)


def _inproj_body(*refs, col_major, n_batch, n_lat_tiles):
    lat_ref, ctx_ref, prev_ref, next_ref, mod_ref, nw_ref, w_ref, cw_ref, cb_ref = refs[:9]
    z_ref, xc_ref, small_ref, qkv_ref, gg_ref, u5_ref, sg_ref = refs[9:]
    b, j = pl.program_id(0), pl.program_id(1)
    is_ctx = j == n_lat_tiles
    jj = jnp.minimum(j, n_lat_tiles - 1)
    x = _load_tile(lat_ref, ctx_ref, col_major, is_ctx, jj)
    m = mod_ref[pl.ds(jnp.where(is_ctx, n_batch, b), 1), :]
    shift, scale = m[:, :D_MODEL], m[:, D_MODEL:2 * D_MODEL]

    def modulated(v):
        v = v * lax.rsqrt(jnp.mean(v * v, axis=-1, keepdims=True) + EPS) * nw_ref[...]
        return (v * (1.0 + scale) + shift).astype(BF16)

    u = modulated(x)

    u_halo = modulated(_load_halo(prev_ref, next_ref, col_major, jj, lat_ref.shape[1]))
    has_prev = jnp.logical_and(jnp.logical_not(is_ctx), j > 0)
    has_next = jnp.logical_and(jnp.logical_not(is_ctx), j < n_lat_tiles - 1)

    z_ref[0] = _dot(u, w_ref[:, C_Z:C_XBC]).astype(z_ref.dtype)
    for c0 in range(0, SSD_CONV_DIM, CONV_COLS):
        cols = slice(C_XBC + c0, C_XBC + c0 + CONV_COLS)
        halo = _dot(u_halo, w_ref[:, cols])
        ext = jnp.concatenate([jnp.where(has_prev, halo[:SUB], 0.0), _dot(u, w_ref[:, cols]),
                               jnp.where(has_next, halo[SUB:], 0.0)], axis=0)
        acc = jnp.broadcast_to(cb_ref[:, c0:c0 + CONV_COLS], (TILE, CONV_COLS))
        for k in range(SSD_CONV_K):
            off = SUB + k - SSD_CONV_K // 2
            acc = acc + cw_ref[k:k + 1, c0:c0 + CONV_COLS] * ext[off:off + TILE, :]
        xc_ref[0, :, c0:c0 + CONV_COLS] = _silu(acc).astype(xc_ref.dtype)
    small_ref[0] = _dot(u, w_ref[:, C_SMALL:C_QKV])
    qkv_ref[0] = _dot(u, w_ref[:, C_QKV:C_GG]).astype(qkv_ref.dtype)
    gg_ref[0] = _dot(u, w_ref[:, C_GG:C_U5]).astype(gg_ref.dtype)
    u5 = _dot(u, w_ref[:, C_U5:C_SG])
    for k in range(S5_WIDTH // LANES):
        u5_ref[0, k] = u5[:, k * LANES:(k + 1) * LANES]
    sg_ref[0] = _dot(u, w_ref[:, C_SG:C_END]).astype(sg_ref.dtype)


def _inproj(lat, ctx, ctx_block, mod_l, norm_w, w_packed, conv_w, conv_b, *, col_major, n_batch, seq, rows):
    n_lat_tiles = seq // TILE
    lt = seq + CTX_LEN
    tok = lambda width: pl.BlockSpec((1, TILE, width), lambda b, j: (b, j, 0))
    full = lambda a: pl.BlockSpec(a.shape, lambda b, j: (0,) * a.ndim, pipeline_mode=pl.Buffered(1))
    shape = lambda width, dtype=STORE: jax.ShapeDtypeStruct((n_batch, lt, width), dtype)
    return pl.pallas_call(
        functools.partial(_inproj_body, col_major=col_major, n_batch=n_batch, n_lat_tiles=n_lat_tiles),
        grid=(n_batch, n_lat_tiles + 1),
        in_specs=_token_specs(n_lat_tiles, rows, col_major, True, ctx_block)
        + _halo_specs(n_lat_tiles, rows, col_major, seq)
        + [full(mod_l), full(norm_w), full(w_packed), full(conv_w), full(conv_b)],
        out_specs=[tok(1024), tok(SSD_CONV_DIM), tok(LANES), tok(1024), tok(512),
                   pl.BlockSpec((1, S5_WIDTH // LANES, TILE, LANES), lambda b, j: (b, 0, j, 0)),
                   tok(512)],
        out_shape=[shape(1024), shape(SSD_CONV_DIM), shape(LANES, F32), shape(1024), shape(512),
                   jax.ShapeDtypeStruct((n_batch, S5_WIDTH // LANES, lt, LANES), F32), shape(512)],
        compiler_params=_params(("arbitrary", "arbitrary")),
        name="inproj_col" if col_major else "inproj_row",
    )(lat, ctx, lat, lat, mod_l, norm_w, w_packed, conv_w, conv_b)


def _ssd_body(xcf_ref, dtf_ref, xcb_ref, dtb_ref, bias_ref, alog_ref, dsk_ref, yf_ref, yb_ref, stf_ref, stb_ref):
    @pl.when(pl.program_id(1) == 0)
    def _():
        stf_ref[...] = jnp.zeros_like(stf_ref)
        stb_ref[...] = jnp.zeros_like(stb_ref)

    chains = []
    for bi in range(xcf_ref.shape[0]):
        chains.append(_ssd_dir(xcf_ref, dtf_ref, bias_ref, alog_ref, dsk_ref, yf_ref, stf_ref, bi, rev=False))
        chains.append(_ssd_dir(xcb_ref, dtb_ref, bias_ref, alog_ref, dsk_ref, yb_ref, stb_ref, bi, rev=True))
    _interleave(*chains)


def _ssd_dir(xc_ref, dt_ref, bias_ref, alog_ref, dsk_ref, y_ref, st_ref, bi, *, rev):
    t = SSD_CHUNK
    d = 1 if rev else 0
    off = SMALL_DT + SSD_HEADS * d
    dt = jax.nn.softplus(dt_ref[bi] + bias_ref[...])
    da = dt * -jnp.exp(alog_ref[...])
    row, col = _iota((t, t), 0), _iota((t, t), 1)
    tri = row <= col if rev else row >= col
    p = _dot_exact_lhs(jnp.where(tri, 1.0, 0.0).astype(BF16), da)
    total = jnp.sum(da, axis=0, keepdims=True)
    e_tot = jnp.exp(total)
    p_t = p.T
    dt_t = dt.T
    w_t = (dt * jnp.exp(total - p)).T
    lo = _iota((1, LANES), 1) < SSD_HEADDIM
    hi = jnp.logical_not(lo)
    pairs_per_group = SSD_HEADS // 2 // 2
    yield
    for g in range(2):
        bg = xc_ref[bi, :, SSD_WIDTH + g * SSD_STATE:SSD_WIDTH + (g + 1) * SSD_STATE]
        cg = xc_ref[bi, :, SSD_WIDTH + (2 + g) * SSD_STATE:SSD_WIDTH + (3 + g) * SSD_STATE]
        cg_b = cg.astype(BF16)
        cb = _dot_nt(cg_b, bg.astype(BF16))
        bg_t = bg.astype(F32).T
        for q in range(pairs_per_group):
            hp = g * pairs_per_group + q
            h0, h1 = off + 2 * hp, off + 2 * hp + 1
            pb0 = jnp.broadcast_to(p[:, h0:h0 + 1], (t, LANES))
            pb1 = jnp.broadcast_to(p[:, h1:h1 + 1], (t, LANES))
            xp = xc_ref[bi, :, hp * LANES:(hp + 1) * LANES]
            st = st_ref[bi, hp]
            acc = jnp.exp(jnp.where(lo, pb0, pb1)) * _dot(cg_b, st.astype(BF16))
            if not rev:
                acc = acc + dsk_ref[:, hp * LANES:(hp + 1) * LANES] * xp.astype(F32)
            st = jnp.where(lo, e_tot[:, h0:h0 + 1], e_tot[:, h1:h1 + 1]) * st
            for h, pb, sel in ((h0, pb0, lo), (h1, pb1, hi)):
                x_h = jnp.where(sel, xp, jnp.zeros_like(xp)).astype(BF16)
                decay = jnp.where(tri, jnp.exp(pb - p_t[h:h + 1, :]), 0.0)
                acc = acc + _dot((cb * decay * dt_t[h:h + 1, :]).astype(BF16), x_h)
                st = st + _dot((bg_t * w_t[h:h + 1, :]).astype(BF16), x_h)
            y_ref[bi, :, hp * LANES:(hp + 1) * LANES] = acc.astype(y_ref.dtype)
            st_ref[bi, hp] = st
            yield


def _ssd_scan(xc, dt, dt_bias, a_log, dsk, *, seq):
    n_batch, total, _ = xc.shape
    nc = total // SSD_CHUNK
    lat_c = seq // SSD_CHUNK
    ctx_c = nc - lat_c
    fwd = lambda i: jnp.where(i < ctx_c, lat_c + i, i - ctx_c)
    bwd = lambda i: nc - 1 - i
    full = lambda a: pl.BlockSpec(a.shape, lambda b, i: (0,) * a.ndim)
    nb = SSD_BATCH if n_batch % SSD_BATCH == 0 else 1
    xc_spec =lambda chunk: pl.BlockSpec((nb, SSD_CHUNK, SSD_CONV_DIM), lambda b, i: (b, chunk(i), 0))
    dt_spec = lambda chunk: pl.BlockSpec((nb, SSD_CHUNK, LANES), lambda b, i: (b, chunk(i), 0))
    y_spec = lambda chunk: pl.BlockSpec((nb, SSD_CHUNK, SSD_WIDTH), lambda b, i: (b, chunk(i), 0))
    y_shape = jax.ShapeDtypeStruct((n_batch, total, SSD_WIDTH), STORE)
    return pl.pallas_call(
        _ssd_body,
        grid=(n_batch // nb, nc),
        in_specs=[xc_spec(fwd), dt_spec(fwd), xc_spec(bwd), dt_spec(bwd),
                  full(dt_bias), full(a_log), full(dsk)],
        out_specs=[y_spec(fwd), y_spec(bwd)],
        out_shape=[y_shape, y_shape],
        scratch_shapes=[pltpu.VMEM((nb, SSD_HEADS // 2, SSD_STATE, LANES), F32)] * 2,
        compiler_params=_params(("arbitrary", "arbitrary")),
        name="ssd_scan",
    )(xc, dt, xc, dt, dt_bias, a_log, dsk)


def _tile_index(d, i, n_lat_tiles):
    return jnp.where(i == 0, n_lat_tiles, jnp.where(d == 0, i - 1, n_lat_tiles - i))


def _gla_body(qkvf_ref, lrf_ref, qkvb_ref, lrb_ref, wlr_ref, blr_ref, of_ref, ob_ref, sf_ref, sb_ref):
    @pl.when(pl.program_id(1) == 0)
    def _():
        sf_ref[...] = jnp.zeros_like(sf_ref)
        sb_ref[...] = jnp.zeros_like(sb_ref)

    chains = []
    for bi in range(qkvf_ref.shape[0]):
        chains.append(_gla_dir(qkvf_ref, lrf_ref, wlr_ref, blr_ref, of_ref, sf_ref, bi, rev=False))
        chains.append(_gla_dir(qkvb_ref, lrb_ref, wlr_ref, blr_ref, ob_ref, sb_ref, bi, rev=True))
    _interleave(*chains)


def _gla_dir(qkv_ref, lr_ref, wlr_ref, blr_ref, o_ref, s_ref, bi, *, rev):
    d = 1 if rev else 0
    t = GLA_CHUNK
    n_chunks = TILE // t
    kw = GLA_HEADS * GLA_DK
    row, col = _iota((TILE, TILE), 0), _iota((TILE, TILE), 1)
    same = row // t == col // t
    tri = jnp.where(same, row - col, 1 if rev else -1)
    tri = tri <= 0 if rev else tri >= 0
    lo = _iota((1, LANES), 1) < GLA_DK
    hi = jnp.logical_not(lo)
    eye = _iota((LANES, LANES), 0) == _iota((LANES, LANES), 1)
    row_lo = _iota((LANES, 1), 0) < GLA_DK

    logit = jnp.dot(lr_ref[bi], wlr_ref[d], precision=HIGHEST, preferred_element_type=F32) + blr_ref[d]
    yield
    g = jax.nn.log_sigmoid(logit) / GLA_TAU
    p = _dot_exact_lhs(jnp.where(tri, 1.0, 0.0).astype(BF16), g, terms=2)
    total = jnp.concatenate(
        [jnp.broadcast_to(p[(c * t if rev else (c + 1) * t - 1):(c * t + 1 if rev else (c + 1) * t), :], (t, kw))
         for c in range(n_chunks)], axis=0)
    yield
    q = qkv_ref[bi, :, 0:kw].astype(F32) * GLA_DK ** -0.5
    k = qkv_ref[bi, :, kw:2 * kw].astype(F32)
    qe = q * jnp.exp(p)
    ke = k * jnp.exp(-p)
    kd = k * jnp.exp(total - p)
    e_tot = jnp.exp(total)
    order = range(n_chunks - 1, -1, -1) if rev else range(n_chunks)
    pairs = range(GLA_HEADS // 2)
    lanes = lambda pr: slice(pr * LANES, (pr + 1) * LANES)
    qms, intra, vs = {}, {}, {}
    for pr in pairs:
        ke_p = ke[:, lanes(pr)].astype(BF16)
        for e, sel in ((0, lo), (1, hi)):
            h = 2 * pr + e
            qms[h] = jnp.where(sel, qe[:, lanes(pr)], 0.0).astype(BF16)
            vs[h] = qkv_ref[bi, :, 2 * kw + h * GLA_DV:2 * kw + (h + 1) * GLA_DV].astype(BF16)
            attn = jnp.where(tri, _dot_nt(qms[h], ke_p), 0.0).astype(BF16)
            yield
            intra[h] = _dot(attn, vs[h])
            yield
    upd, dec = {}, {}
    for pr in pairs:
        for c in order:
            rows = slice(c * t, (c + 1) * t)
            kd_t = kd[rows, lanes(pr)].T.astype(BF16)
            upd[pr, c] = jnp.where(row_lo, _dot(kd_t, vs[2 * pr][rows]), _dot(kd_t, vs[2 * pr + 1][rows]))
            dec[pr, c] = jnp.sum(jnp.where(eye, e_tot[c * t:c * t + 1, lanes(pr)], 0.0), axis=1, keepdims=True)
        yield
    for pr in pairs:
        s = s_ref[bi, pr]
        for c in order:
            rows = slice(c * t, (c + 1) * t)
            s_b = s.astype(BF16)
            for h in (2 * pr, 2 * pr + 1):
                o_ref[bi, rows, h * GLA_DV:(h + 1) * GLA_DV] = (intra[h][rows] + _dot(qms[h][rows], s_b)).astype(o_ref.dtype)
            s = dec[pr, c] * s + upd[pr, c]
            yield
        s_ref[bi, pr] = s


def _gla_scan(qkv, lr, w_lr, b_lr, *, seq):
    n_batch, total, _ = qkv.shape
    n_lat_tiles = seq // TILE
    tile = lambda d: (lambda i: _tile_index(d, i, n_lat_tiles))
    full = lambda a: pl.BlockSpec(a.shape, lambda b, i: (0,) * a.ndim)
    nb = GLA_BATCH if n_batch % GLA_BATCH == 0 else 1
    qkv_spec =lambda d: pl.BlockSpec((nb, TILE, 1024), lambda b, i: (b, tile(d)(i), 0))
    lr_spec = lambda d: pl.BlockSpec((nb, TILE, LANES), lambda b, i: (b, tile(d)(i), 0))
    o_spec = lambda d: pl.BlockSpec((nb, TILE, GLA_WIDTH), lambda b, i: (b, tile(d)(i), 0))
    o_shape = jax.ShapeDtypeStruct((n_batch, total, GLA_WIDTH), STORE)
    return pl.pallas_call(
        _gla_body,
        grid=(n_batch // nb, n_lat_tiles + 1),
        in_specs=[qkv_spec(0), lr_spec(0), qkv_spec(1), lr_spec(1), full(w_lr), full(b_lr)],
        out_specs=[o_spec(0), o_spec(1)],
        out_shape=[o_shape, o_shape],
        scratch_shapes=[pltpu.VMEM((nb, GLA_HEADS // 2, LANES, GLA_DV), F32)] * 2,
        compiler_params=_params(("arbitrary", "arbitrary")),
        name="gla_scan",
    )(qkv, lr, qkv, lr, w_lr, b_lr)


S5_ROWS = TILE // S5_T
SLABS_IN_FLIGHT = 4


def _s5_prep_body(ls_ref, lrr_ref, lir_ref, lrc_ref, lic_ref, p1_ref, p2_ref, q1_ref, q2_ref,
                  g_ref, win_ref, wout_ref, ar_ref, ai_ref):
    fwd = pl.program_id(0) % 2 == 0
    n = LANES
    step = jnp.exp(ls_ref[0, 0])
    lo = _iota((1, n), 1) < S5_STATE
    top = _iota((n, 1), 0) < S5_STATE

    lre, lim = lrr_ref[0, 0], lir_ref[0, 0]
    lr, ang = lre * step, lim * step
    er = jnp.exp(lr)
    nr, ni = er * jnp.cos(ang) - 1.0, er * jnp.sin(ang)
    den = lre * lre + lim * lim
    cr, ci = (nr * lre + ni * lim) / den, (ni * lre - nr * lim) / den
    p1, p2 = p1_ref[0, 0], p2_ref[0, 0]
    bbr, bbi = cr * p1 - ci * p2, cr * p2 + ci * p1
    s_idx = _iota((n, 1), 0) // S5_GROUP
    pos_s = jnp.where(fwd, s_idx, S5_T - 1 - s_idx)
    xs = (S5_T - 1 - pos_s).astype(F32)
    e = jnp.exp(xs * lr)
    l_re, l_im = e * jnp.cos(xs * ang), e * jnp.sin(xs * ang)
    w_in = jnp.where(lo, bbr * l_re - bbi * l_im, bbr * l_im + bbi * l_re)
    win_ref[0, 0] = w_in.astype(BF16)

    lrc, angc = lrc_ref[0, 0] * step, lic_ref[0, 0] * step
    q1, q2 = q1_ref[0, 0], q2_ref[0, 0]
    t_idx = _iota((1, n), 1) // S5_GROUP
    pos_t = jnp.where(fwd, t_idx, S5_T - 1 - t_idx)

    def out_op(xt):
        ex = jnp.exp(lrc * xt)
        o_re, o_im = ex * jnp.cos(angc * xt), ex * jnp.sin(angc * xt)
        return jnp.where(top, q1 * o_re - q2 * o_im, -(q1 * o_im + q2 * o_re))

    wout_ref[0, 0] = out_op((pos_t + 1).astype(F32)).astype(BF16)
    g = jnp.dot(w_in, out_op((pos_t - (S5_T - 1)).astype(F32)), precision=HIGHEST,
                preferred_element_type=F32)
    g_ref[0, 0] = jnp.where(pos_t >= pos_s, g, 0.0).astype(BF16)

    r_idx = _iota((S5_ROWS, 1), 0)
    xe = (S5_T * (jnp.where(fwd, r_idx, S5_ROWS - 1 - r_idx) + 1)).astype(F32)
    ee = jnp.exp(xe * lr)
    a_im = ee * jnp.sin(xe * ang)
    ar_ref[0, 0] = ee * jnp.cos(xe * ang)
    ai_ref[0, 0] = jnp.where(lo, -a_im, a_im)


def _s5_prep(lam_re, lam_im, log_step, b_re, b_im, c_re, c_im):
    n = LANES
    ld = 2 * lam_re.shape[0]
    dup_l = lambda a: jnp.concatenate([a, a], axis=-1)
    dup_r = lambda a: jnp.concatenate([a, a], axis=-2)
    merge = lambda a: a.reshape((ld,) + a.shape[2:])
    lrr, lir = dup_l(merge(lam_re))[:, :, None, :], dup_l(merge(lam_im))[:, :, None, :]
    lrc, lic = lrr.reshape(ld, S5_GROUPS, n, 1), lir.reshape(ld, S5_GROUPS, n, 1)
    ls = log_step.reshape(ld, S5_GROUPS, 1, 1)
    tile_rows = lambda a: dup_l(jnp.tile(jnp.swapaxes(a, -1, -2), (1, 1, S5_T, 1)))
    p1, p2 = tile_rows(b_re), tile_rows(b_im)
    tile_cols = lambda a: dup_r(jnp.tile(jnp.swapaxes(merge(a), -1, -2), (1, 1, 1, S5_T)))
    q1, q2 = tile_cols(c_re), tile_cols(c_im)
    dg = lambda *blk: pl.BlockSpec((1, 1) + blk, lambda d, g: (d, g, 0, 0))
    gg = lambda *blk: pl.BlockSpec((1, 1) + blk, lambda d, g: (d // 2, g, 0, 0))
    out = lambda rows, dtype: jax.ShapeDtypeStruct((ld, S5_GROUPS, rows, n), dtype)
    return pl.pallas_call(
        _s5_prep_body,
        grid=(ld, S5_GROUPS),
        in_specs=[dg(1, 1), dg(1, n), dg(1, n), dg(n, 1), dg(n, 1), gg(n, n), gg(n, n), dg(n, n), dg(n, n)],
        out_specs=[dg(n, n), dg(n, n), dg(n, n), dg(S5_ROWS, n), dg(S5_ROWS, n)],
        out_shape=[out(n, BF16), out(n, BF16), out(n, BF16), out(S5_ROWS, F32), out(S5_ROWS, F32)],
        compiler_params=_params(("arbitrary", "arbitrary")),
        name="s5_prep",
    )(ls, lrr, lir, lrc, lic, p1, p2, q1, q2)


def _s5_pair_ops(g, win, wout, ar, ai):
    hs = S5_STATE
    cat = jnp.concatenate
    pair = lambda a: (a[:, 0::2], a[:, 1::2])
    g0, g1 = pair(g)
    i0, i1 = pair(win)
    o0, o1 = pair(wout)
    z = jnp.zeros_like(g0)
    zh, zr = z[..., :hs], z[..., :hs, :]
    wa = cat([cat([g0, z, i0[..., :hs], zh, i0[..., hs:], zh], -1),
              cat([z, g1, zh, i1[..., :hs], zh, i1[..., hs:]], -1)], -2)
    wo = cat([cat([o0[..., :hs, :], zr], -1), cat([zr, o1[..., :hs, :]], -1),
              cat([o0[..., hs:, :], zr], -1), cat([zr, o1[..., hs:, :]], -1)], -2)
    a0, a1 = pair(ar)
    b0, b1 = pair(ai)
    return wa, wo, cat([a0[..., :hs], a1[..., :hs]], -1), cat([b0[..., hs:], b1[..., hs:]], -1)


def _block_transpose(arrs):
    n = len(arrs)
    block = _iota((1, LANES), 1) // S5_GROUP
    arrs = list(arrs)
    s = n // 2
    while s >= 1:
        upper = (block // s) % 2 == 1
        nxt = list(arrs)
        for i in range(n):
            if (i // s) % 2 == 0:
                a, b = arrs[i], arrs[i + s]
                nxt[i] = jnp.where(upper, pltpu.roll(b, s * S5_GROUP, 1), a)
                nxt[i + s] = jnp.where(upper, b, pltpu.roll(a, LANES - s * S5_GROUP, 1))
        arrs = nxt
        s //= 2
    return arrs


def _s5_body(uf_ref, ub_ref, wa_ref, wo_ref, are_ref, aim_ref, yf_ref, yb_ref, *h_refs):
    @pl.when(pl.program_id(1) == 0)
    def _():
        for h_ref in h_refs:
            h_ref[...] = jnp.zeros_like(h_ref)

    ops = (wa_ref, wo_ref, are_ref, aim_ref)
    slabs = []
    for bi in range(uf_ref.shape[0]):
        slabs.append(_s5_dir(uf_ref, *ops, yf_ref, *h_refs[:2], bi, rev=False))
        slabs.append(_s5_dir(ub_ref, *ops, yb_ref, *h_refs[2:], bi, rev=True))
    for k0 in range(0, S5_WIDTH // LANES, SLABS_IN_FLIGHT):
        _interleave(*[slab(k) for k in range(k0, k0 + SLABS_IN_FLIGHT) for slab in slabs])


def _s5_dir(u_ref, wa_ref, wo_ref, are_ref, aim_ref, y_ref, hre_ref, him_ref, bi, *, rev):
    d = 1 if rev else 0
    rws = S5_ROWS
    row = _iota((rws, 1), 0)
    pos = rws - 1 - row if rev else row
    per_slab = LANES // S5_GROUP
    pairs = per_slab // 2

    def earlier(v, k):
        return pltpu.roll(v, rws - k if rev else k, 0)

    def slab(k):
        parts = [u_ref[bi, k, pl.ds(t, rws, stride=S5_T), :] for t in range(S5_T)]
        h_all_re, h_all_im = hre_ref[bi, k], him_ref[bi, k]
        xs = _block_transpose(parts)
        yield
        first = []
        for q in range(pairs):
            x2 = jnp.concatenate([xs[2 * q], xs[2 * q + 1]], axis=1).astype(BF16)
            first.append(_dot(x2, wa_ref[d, k * pairs + q]))
            yield
        ys, new_re, new_im = [], [], []
        for q in range(pairs):
            pr = k * pairs + q
            res = first[q]
            y_intra, h_re, h_im = res[:, :2 * LANES], res[:, 2 * LANES:3 * LANES], res[:, 3 * LANES:]
            span = 1
            while span < SUB:
                idx = rws - span if rev else span - 1
                a_re, a_im = are_ref[d, pr, idx:idx + 1, :], aim_ref[d, pr, idx:idx + 1, :]
                s_re = jnp.where(pos % SUB >= span, earlier(h_re, span), 0.0)
                s_im = jnp.where(pos % SUB >= span, earlier(h_im, span), 0.0)
                h_re, h_im = h_re + (a_re * s_re - a_im * s_im), h_im + (a_re * s_im + a_im * s_re)
                span *= 2
            c_re, c_im = h_all_re[SUB * q:SUB * q + 1, :], h_all_im[SUB * q:SUB * q + 1, :]
            in_re, in_im = jnp.broadcast_to(c_re, (rws, LANES)), jnp.broadcast_to(c_im, (rws, LANES))
            first_block = slice(rws - SUB, rws) if rev else slice(0, SUB)
            t_re, t_im = are_ref[d, pr, first_block, :], aim_ref[d, pr, first_block, :]
            blocks_re, blocks_im = {}, {}
            for v in (range(rws // SUB - 1, -1, -1) if rev else range(rws // SUB)):
                blk = slice(SUB * v, SUB * (v + 1))
                b_re = h_re[blk] + (t_re * c_re - t_im * c_im)
                b_im = h_im[blk] + (t_re * c_im + t_im * c_re)
                blocks_re[v], blocks_im[v] = b_re, b_im
                edge = 0 if rev else SUB - 1
                c_re, c_im = b_re[edge:edge + 1, :], b_im[edge:edge + 1, :]
            h_re = jnp.concatenate([blocks_re[v] for v in range(rws // SUB)], axis=0)
            h_im = jnp.concatenate([blocks_im[v] for v in range(rws // SUB)], axis=0)
            p_re = jnp.where(pos == 0, in_re, earlier(h_re, 1))
            p_im = jnp.where(pos == 0, in_im, earlier(h_im, 1))
            new_re.append(jnp.broadcast_to(c_re, (SUB, LANES)))
            new_im.append(jnp.broadcast_to(c_im, (SUB, LANES)))
            y2 = y_intra + _dot(jnp.concatenate([p_re, p_im], axis=1).astype(BF16), wo_ref[d, pr])
            ys += [y2[:, :LANES], y2[:, LANES:]]
            yield
        hre_ref[bi, k] = jnp.concatenate(new_re, axis=0)
        him_ref[bi, k] = jnp.concatenate(new_im, axis=0)
        for t, out in enumerate(_block_transpose(ys)):
            y_ref[bi, k, pl.ds(t, rws, stride=S5_T), :] = out

    return slab


def _s5_scan(u5, ops, *, seq, layer):
    n_batch, slabs, total, _ = u5.shape
    n_lat_tiles = seq // TILE
    nb = S5_BATCH if n_batch % S5_BATCH == 0 else 1
    tok =lambda d: pl.BlockSpec((nb, slabs, TILE, LANES), lambda b, i: (b, 0, _tile_index(d, i, n_lat_tiles), 0))
    full = lambda a: pl.BlockSpec((2,) + a.shape[1:], lambda b, i: (layer, 0, 0, 0),
                                  pipeline_mode=pl.Buffered(1))
    pairs_per_slab = LANES // S5_GROUP // 2
    y_shape = jax.ShapeDtypeStruct((n_batch, slabs, total, LANES), F32)
    return pl.pallas_call(
        _s5_body,
        grid=(n_batch // nb, n_lat_tiles + 1),
        in_specs=[tok(0), tok(1)] + [full(a) for a in ops],
        out_specs=[tok(0), tok(1)],
        out_shape=[y_shape, y_shape],
        scratch_shapes=[pltpu.VMEM((nb, slabs, SUB * pairs_per_slab, LANES), F32)] * 4,
        compiler_params=_params(("arbitrary", "arbitrary")),
        name="s5_scan",
    )(u5, u5, *ops)


def _outproj_body(*refs, col_major, last, n_batch, n_lat_tiles):
    n_tok = 1 if last else 2
    lat_ref = refs[0]
    ctx_ref = None if last else refs[1]
    (mod_ref, yf_ref, yb_ref, z_ref, of_ref, ob_ref, gg_ref, sf_ref, sb_ref, u5_ref, sg_ref,
     snw_ref, gnw_ref, s5d_ref, gluw_ref, glub_ref, wout_ref, fnw_ref, o_ref) = refs[n_tok:]
    b, j = pl.program_id(0), pl.program_id(1)
    is_ctx = j == n_lat_tiles
    res = _load_tile(lat_ref, ctx_ref, col_major, is_ctx, jnp.minimum(j, n_lat_tiles - 1))
    gate = mod_ref[pl.ds(jnp.where(is_ctx, n_batch, b), 1), 2 * D_MODEL:3 * D_MODEL]

    def rms(v):
        return v * lax.rsqrt(jnp.mean(v * v, axis=-1, keepdims=True) + EPS)

    y = (yf_ref[0].astype(F32) + yb_ref[0].astype(F32)) * _silu(z_ref[0].astype(F32))
    half = SSD_WIDTH // 2
    ssd = jnp.concatenate([rms(y[:, :half]), rms(y[:, half:])], axis=1) * snw_ref[...]
    out = _dot(ssd.astype(BF16), wout_ref[0:SSD_WIDTH, :])

    o = of_ref[0].astype(F32) + ob_ref[0].astype(F32)
    gla = jnp.concatenate([rms(o[:, h * GLA_DV:(h + 1) * GLA_DV]) for h in range(GLA_HEADS)], axis=1)
    gla = gla * gnw_ref[...] * _silu(gg_ref[0].astype(F32))
    out = out + _dot(gla.astype(BF16), wout_ref[SSD_WIDTH:SSD_WIDTH + GLA_WIDTH, :])

    slabs = range(S5_WIDTH // LANES)
    y5 = jnp.concatenate([sf_ref[0, k] + sb_ref[0, k] for k in slabs], axis=1)
    u5 = jnp.concatenate([u5_ref[0, k] for k in slabs], axis=1)
    y5 = jax.nn.gelu(y5 + s5d_ref[...] * u5)
    pr = _dot(y5.astype(BF16), gluw_ref[...]) + glub_ref[...]
    s5 = pr[:, :S5_WIDTH] * jax.nn.sigmoid(pr[:, S5_WIDTH:]) * _silu(sg_ref[0].astype(F32))
    out = out + _dot(s5.astype(BF16), wout_ref[SSD_WIDTH + GLA_WIDTH:, :])

    h_new = res + gate * out
    if last:
        h_new = rms(h_new) * fnw_ref[...]
    if col_major:
        rows = o_ref.shape[1]
        base = _col_base(j, rows)
        for k in range(TILE // rows):
            o_ref[0, :, base + k, :] = h_new[k * rows:(k + 1) * rows, :]
    else:
        o_ref[0] = h_new


def _outproj(lat, ctx, ctx_block, mod_l, y_ssd, z, o_gla, gg, y_s5, u5, sg, ssd_norm_w, gla_norm_w, s5_d,
             glu_w, glu_b, w_out, final_norm_w, *, col_major, last, n_batch, seq, rows):
    n_lat_tiles = seq // TILE
    n_tiles = n_lat_tiles if last else n_lat_tiles + 1
    tok = lambda width: pl.BlockSpec((1, TILE, width), lambda b, j: (b, j, 0))
    both = lambda width: [tok(width), tok(width)]
    slabs = S5_WIDTH // LANES
    s5_tok = pl.BlockSpec((1, slabs, TILE, LANES), lambda b, j: (b, 0, j, 0))
    full = lambda a: pl.BlockSpec(a.shape, lambda b, j: (0,) * a.ndim, pipeline_mode=pl.Buffered(1))
    small = [ssd_norm_w, gla_norm_w, s5_d, glu_w, glu_b, w_out, final_norm_w]
    tokens = (lat,) if last else (lat, ctx)
    if col_major:
        per_block = COL_W // (TILE // rows)
        out_spec = pl.BlockSpec((1, rows, COL_W, D_MODEL), lambda b, j: (b, 0, j // per_block, 0))
        out_shape = jax.ShapeDtypeStruct((n_batch, rows, seq // rows, D_MODEL), F32)
    else:
        out_spec = tok(D_MODEL)
        out_shape = jax.ShapeDtypeStruct((n_batch, n_tiles * TILE, D_MODEL), F32)
    return pl.pallas_call(
        functools.partial(_outproj_body, col_major=col_major, last=last, n_batch=n_batch,
                          n_lat_tiles=n_lat_tiles),
        grid=(n_batch, n_tiles),
        in_specs=_token_specs(n_lat_tiles, rows, col_major, not last, ctx_block)
        + [full(mod_l)] + both(SSD_WIDTH) + [tok(SSD_WIDTH)] + both(GLA_WIDTH) + [tok(GLA_WIDTH)]
        + [s5_tok, s5_tok, s5_tok, tok(S5_WIDTH)]
        + [full(a) for a in small],
        out_specs=out_spec,
        out_shape=out_shape,
        compiler_params=_params(("arbitrary", "arbitrary")),
        name="outproj_col" if col_major else "outproj_row",
    )(*tokens, mod_l, *y_ssd, z, *o_gla, gg, *y_s5, u5, sg, *small)


def _pack_w_in(w):
    sizes = (1024, SSD_CONV_DIM, 32, 256, 256, 512, 512, 32, 512, 512)
    parts, start = [], 0
    for size in sizes:
        parts.append(w[..., start:start + size])
        start += size
    z, xbc, dt, q, k, v, gg, lr, u5, sg = parts
    small = jnp.concatenate([dt, lr], axis=-1)
    small = jnp.pad(small, ((0, 0),) * (w.ndim - 1) + ((0, LANES - small.shape[-1]),))
    return jnp.concatenate([z, xbc, small, q, k, v, gg, u5, sg], axis=-1).astype(BF16)


def kernel(x, c, ctx, c_ctx, norm_w, mod_w, mod_b, w_in, w_out, ssd_conv_w, ssd_conv_b, ssd_a_log, ssd_dt_bias, ssd_d, ssd_norm_w, gla_w_lr, gla_b_lr, gla_norm_w, s5_lam_re, s5_lam_im, s5_log_step, s5_b_re, s5_b_im, s5_c_re, s5_c_im, s5_d, s5_glu_w, s5_glu_b, final_norm_w):
    n_batch, seq, d_model = x.shape
    depth = norm_w.shape[0]
    rows = seq // GRID_W
    assert d_model == D_MODEL and ctx.shape[1] == CTX_LEN == TILE
    assert seq % TILE == 0 and TILE % rows == 0
    assert depth == 2, "layer 0 row-major with context update, layer 1 column-major and last"
    n_lat_tiles = seq // TILE

    mod_rows = -(-(n_batch + 1) // 8) * 8
    cc = jnp.zeros((mod_rows, d_model), F32).at[:n_batch].set(c).at[n_batch].set(c_ctx)
    mod = _modulation(cc, mod_w, mod_b)

    w_in_packed, w_out_b, glu_w_b = _pack_w_in(w_in), w_out.astype(BF16), s5_glu_w.astype(BF16)
    s5_ops = _s5_pair_ops(*_s5_prep(s5_lam_re, s5_lam_im, s5_log_step, s5_b_re, s5_b_im, s5_c_re, s5_c_im))

    lat, ctx_arr, ctx_block = x, ctx, 0
    for l in range(depth):
        col_major = l % 2 == 1
        last = l == depth - 1
        if col_major:
            lat_in = lat.reshape(n_batch, lat.shape[1] // GRID_W, GRID_W, d_model)
        else:
            lat_in = lat
        z, xc, small, qkv, gg, u5, sg = _inproj(
            lat_in, ctx_arr, ctx_block, mod[l], norm_w[l].reshape(1, -1), w_in_packed[l],
            ssd_conv_w[l], ssd_conv_b[l].reshape(1, -1),
            col_major=col_major, n_batch=n_batch, seq=seq, rows=rows)
        pad_h = lambda a: jnp.pad(a.reshape(1, -1), ((0, 0), (SMALL_DT, LANES - SMALL_DT - a.size)))
        y_ssd = _ssd_scan(xc, small, pad_h(ssd_dt_bias[l]), pad_h(ssd_a_log[l]),
                          jnp.repeat(ssd_d[l], SSD_HEADDIM).reshape(1, -1), seq=seq)

        w_lr = jnp.stack([jnp.pad(gla_w_lr[l, d], ((SMALL_LR + GLA_RANK * d, LANES - SMALL_LR - GLA_RANK * (d + 1)), (0, 0)))
                          for d in (0, 1)])
        o_gla = _gla_scan(qkv, small, w_lr, gla_b_lr[l].reshape(2, 1, -1), seq=seq)

        y_s5 = _s5_scan(u5, s5_ops, seq=seq, layer=l)

        out = _outproj(
            lat_in, ctx_arr, ctx_block, mod[l], y_ssd, z, o_gla, gg, y_s5, u5, sg,
            ssd_norm_w[l].reshape(1, -1), jnp.tile(gla_norm_w[l], GLA_HEADS).reshape(1, -1),
            s5_d[l].reshape(1, -1), glu_w_b[l], s5_glu_b[l].reshape(1, -1),
            w_out_b[l], final_norm_w.reshape(1, -1),
            col_major=col_major, last=last, n_batch=n_batch, seq=seq, rows=rows)
        if last:
            return out.reshape(n_batch, seq, d_model)
        lat, ctx_arr, ctx_block = out, out, n_lat_tiles
```
